```python
import jax, jax.numpy as jnp
from jax import lax
import numpy as np

D_MODEL = 1024
BATCH = 4
SEQ = 4096
DEPTH = 2

N_EVEN = (DEPTH + 1) // 2
N_ODD = DEPTH // 2

LRU_WIDTH = D_MODEL // 2
LRU_HEADS = 8
LRU_HEAD_DIM = LRU_WIDTH // LRU_HEADS
LRU_CONV = 4
LRU_C = 8.0
LRU_MIN_RAD = 0.9
LRU_MAX_RAD = 0.999
SC_WIDTH = D_MODEL // 2
SC_CONV = 3
EVEN_IN = 2 * LRU_WIDTH + 3 * SC_WIDTH

SGU_WIDTH = D_MODEL // 2
SGU_HEADS = 8
SGU_HEAD_DIM = SGU_WIDTH // SGU_HEADS
CHUNK = 128
FOX_HEADS = 8
FOX_HEAD_DIM = 64
FOX_WIDTH = FOX_HEADS * FOX_HEAD_DIM
Q_BLOCK = 128
ODD_IN = 2 * SGU_WIDTH + 3 * FOX_WIDTH + FOX_HEADS

D_FF = 2816
FFN_CONV = 3
EPS = 1e-6

kernel_name = "hybrid_rglru_shortconv_sgu_fox_block"


def rmsnorm(x, g):
    xf = x.astype(jnp.float32)
    y = xf * lax.rsqrt(jnp.mean(xf * xf, axis=-1, keepdims=True) + EPS)
    return (y * g.astype(jnp.float32)).astype(x.dtype)


def causal_depthwise_conv(x, w, b):
    k_width, ch = w.shape
    out = lax.conv_general_dilated(
        x, w[:, None, :].astype(x.dtype), window_strides=(1,),
        padding=[(k_width - 1, 0)], dimension_numbers=('NWC', 'WIO', 'NWC'),
        feature_group_count=ch)
    return out + b.astype(x.dtype)


def rg_lru(x, w_a, b_a, w_x, b_x, lam):
    bsz, s, w = x.shape
    xh = x.reshape(bsz, s, LRU_HEADS, LRU_HEAD_DIM)
    r = jax.nn.sigmoid((jnp.einsum('bshi,hij->bshj', xh, w_a).reshape(bsz, s, w) + b_a).astype(jnp.float32))
    i = jax.nn.sigmoid((jnp.einsum('bshi,hij->bshj', xh, w_x).reshape(bsz, s, w) + b_x).astype(jnp.float32))
    log_a = -LRU_C * r * jax.nn.softplus(-lam.astype(jnp.float32))
    a = jnp.exp(log_a)
    u = jnp.sqrt(-jnp.expm1(2.0 * log_a)) * (i * x.astype(jnp.float32))

    def combine(left, right):
        a_l, h_l = left
        a_r, h_r = right
        return a_l * a_r, a_r * h_l + h_r

    _, h = lax.associative_scan(combine, (a, u), axis=1)
    return h.astype(x.dtype)


def even_mixer(x, w_in, lru_conv_w, lru_conv_b, lru_wa, lru_ba, lru_wx, lru_bx, lru_lambda,
               sconv_w, sconv_b, w_out):
    p = x @ w_in
    W, C = LRU_WIDTH, SC_WIDTH
    xa, ga, c_pre, b_post, vb = jnp.split(p, [W, 2 * W, 2 * W + C, 2 * W + 2 * C], axis=-1)
    xa = causal_depthwise_conv(xa, lru_conv_w, lru_conv_b)
    ya = rg_lru(xa, lru_wa, lru_ba, lru_wx, lru_bx, lru_lambda) * jax.nn.gelu(ga)
    yb = b_post * causal_depthwise_conv(c_pre * vb, sconv_w, sconv_b)
    return jnp.concatenate([ya, yb], axis=-1) @ w_out


def chunked_spatial_gating(u, g, g_norm, w_s, b_s):
    bsz, s, _ = u.shape
    n_chunks = s // CHUNK
    gv = rmsnorm(g.reshape(bsz, s, SGU_HEADS, SGU_HEAD_DIM), g_norm.reshape(SGU_HEADS, SGU_HEAD_DIM))
    gv = gv.reshape(bsz, n_chunks, CHUNK, SGU_HEADS, SGU_HEAD_DIM)
    w_causal = jnp.tril(w_s)
    mixed = jnp.einsum('gts,bnsgc->bntgc', w_causal, gv) + b_s.T[:, :, None]
    return u * mixed.reshape(bsz, s, SGU_WIDTH)


def forgetting_attention(q, k, v, f_logit, b_f):
    bsz, s, _ = q.shape
    nb = s // Q_BLOCK

    def heads(t):
        return t.reshape(bsz, s, FOX_HEADS, FOX_HEAD_DIM).transpose(0, 2, 1, 3)

    q, k, v = heads(q), heads(k), heads(v)
    log_f = jax.nn.log_sigmoid(f_logit.astype(jnp.float32) + b_f.astype(jnp.float32))
    c = jnp.cumsum(log_f, axis=1).transpose(0, 2, 1)
    q_blocks = q.reshape(bsz, FOX_HEADS, nb, Q_BLOCK, FOX_HEAD_DIM).transpose(2, 0, 1, 3, 4)
    c_blocks = c.reshape(bsz, FOX_HEADS, nb, Q_BLOCK).transpose(2, 0, 1, 3)
    starts = jnp.arange(nb) * Q_BLOCK
    key_pos = jnp.arange(s)
    scale = FOX_HEAD_DIM ** -0.5

    def block(args):
        qb, cb, start = args
        logits = (jnp.einsum('bhqd,bhkd->bhqk', qb, k).astype(jnp.float32) * scale
                  + cb[..., None] - c[:, :, None, :])
        q_pos = start + jnp.arange(Q_BLOCK)
        logits = jnp.where(key_pos[None, :] <= q_pos[:, None], logits, -jnp.inf)
        p = jax.nn.softmax(logits, axis=-1)
        return jnp.einsum('bhqk,bhkd->bhqd', p.astype(v.dtype), v)

    out = lax.map(block, (q_blocks, c_blocks, starts))
    return out.transpose(1, 0, 3, 2, 4).reshape(bsz, s, FOX_WIDTH)


def odd_mixer(x, w_in, sgu_norm, sgu_w, sgu_b, fox_bf, w_out):
    p = x @ w_in
    Z, F = 2 * SGU_WIDTH, FOX_WIDTH
    z, q, k, v, f = jnp.split(p, [Z, Z + F, Z + 2 * F, Z + 3 * F], axis=-1)
    z = jax.nn.gelu(z)
    u, g = jnp.split(z, 2, axis=-1)
    yc = chunked_spatial_gating(u, g, sgu_norm, sgu_w, sgu_b)
    yd = forgetting_attention(q, k, v, f, fox_bf)
    return jnp.concatenate([yc, yd], axis=-1) @ w_out


def conv_glu_ffn(x, w_up, conv_w, conv_b, w_down):
    h = causal_depthwise_conv(x @ w_up, conv_w, conv_b)
    gate, val = jnp.split(h, 2, axis=-1)
    return (jax.nn.silu(gate) * val) @ w_down


def setup_inputs(seed: int = 0) -> dict:
    key = jax.random.key(seed)
    ks = jax.random.split(key, 32)
    f32 = jnp.float32

    def dense(k, shape, fan_in):
        return jax.random.normal(k, shape, f32) * fan_in ** -0.5

    def gain(k, shape):
        return 1.0 + 0.05 * jax.random.normal(k, shape, f32)

    def bias(k, shape):
        return 0.1 * jax.random.normal(k, shape, f32)

    a_c = jax.random.uniform(ks[9], (N_EVEN, LRU_WIDTH), f32, LRU_MIN_RAD, LRU_MAX_RAD)
    s_base = a_c ** (1.0 / LRU_C)
    lru_lambda = jnp.log(s_base) - jnp.log1p(-s_base)

    return {
        "x": jax.random.normal(ks[0], (BATCH, SEQ, D_MODEL), f32),
        "mix0_norm": gain(ks[1], (N_EVEN, D_MODEL)),
        "mix0_w_in": dense(ks[2], (N_EVEN, D_MODEL, EVEN_IN), D_MODEL),
        "lru_conv_w": dense(ks[3], (N_EVEN, LRU_CONV, LRU_WIDTH), LRU_CONV),
        "lru_conv_b": bias(ks[4], (N_EVEN, LRU_WIDTH)),
        "lru_wa": dense(ks[5], (N_EVEN, LRU_HEADS, LRU_HEAD_DIM, LRU_HEAD_DIM), LRU_HEAD_DIM),
        "lru_ba": bias(ks[6], (N_EVEN, LRU_WIDTH)),
        "lru_wx": dense(ks[7], (N_EVEN, LRU_HEADS, LRU_HEAD_DIM, LRU_HEAD_DIM), LRU_HEAD_DIM),
        "lru_bx": bias(ks[8], (N_EVEN, LRU_WIDTH)),
        "lru_lambda": lru_lambda,
        "sconv_w": dense(ks[10], (N_EVEN, SC_CONV, SC_WIDTH), SC_CONV),
        "sconv_b": bias(ks[11], (N_EVEN, SC_WIDTH)),
        "mix0_w_out": dense(ks[12], (N_EVEN, LRU_WIDTH + SC_WIDTH, D_MODEL), LRU_WIDTH + SC_WIDTH),
        "mix1_norm": gain(ks[13], (N_ODD, D_MODEL)),
        "mix1_w_in": dense(ks[14], (N_ODD, D_MODEL, ODD_IN), D_MODEL),
        "sgu_norm": gain(ks[15], (N_ODD, SGU_WIDTH)),
        "sgu_w": dense(ks[16], (N_ODD, SGU_HEADS, CHUNK, CHUNK), CHUNK),
        "sgu_b": 1.0 + bias(ks[17], (N_ODD, SGU_HEADS, CHUNK)),
        "fox_bf": bias(ks[18], (N_ODD, FOX_HEADS)),
        "mix1_w_out": dense(ks[19], (N_ODD, SGU_WIDTH + FOX_WIDTH, D_MODEL), SGU_WIDTH + FOX_WIDTH),
        "ffn_norm": gain(ks[20], (DEPTH, D_MODEL)),
        "ffn_up": dense(ks[21], (DEPTH, D_MODEL, 2 * D_FF), D_MODEL),
        "ffn_conv_w": dense(ks[22], (DEPTH, FFN_CONV, 2 * D_FF), FFN_CONV),
        "ffn_conv_b": bias(ks[23], (DEPTH, 2 * D_FF)),
        "ffn_down": dense(ks[24], (DEPTH, D_FF, D_MODEL), D_FF),
        "final_norm": gain(ks[25], (D_MODEL,)),
    }


def reference(x, mix0_norm, mix0_w_in, lru_conv_w, lru_conv_b, lru_wa, lru_ba, lru_wx, lru_bx,
              lru_lambda, sconv_w, sconv_b, mix0_w_out, mix1_norm, mix1_w_in, sgu_norm, sgu_w,
              sgu_b, fox_bf, mix1_w_out, ffn_norm, ffn_up, ffn_conv_w, ffn_conv_b, ffn_down,
              final_norm):
    h = x
    for layer in range(DEPTH):
        i = layer // 2
        if layer % 2 == 0:
            h = h + even_mixer(rmsnorm(h, mix0_norm[i]), mix0_w_in[i], lru_conv_w[i], lru_conv_b[i],
                               lru_wa[i], lru_ba[i], lru_wx[i], lru_bx[i], lru_lambda[i],
                               sconv_w[i], sconv_b[i], mix0_w_out[i])
        else:
            h = h + odd_mixer(rmsnorm(h, mix1_norm[i]), mix1_w_in[i], sgu_norm[i], sgu_w[i],
                              sgu_b[i], fox_bf[i], mix1_w_out[i])
        h = h + conv_glu_ffn(rmsnorm(h, ffn_norm[layer]), ffn_up[layer], ffn_conv_w[layer],
                             ffn_conv_b[layer], ffn_down[layer])
    return rmsnorm(h, final_norm)
```

```python
import functools

import jax
import jax.numpy as jnp
from jax import lax
from jax.experimental import pallas as pl
from jax.experimental.pallas import tpu as pltpu

F32 = jnp.float32
BF16 = jnp.bfloat16

EPS = 1e-6
D_MODEL = 1024
D_FF = 2816
FFN_TAPS = 3
HALO = 8
VMEM_LIMIT_BYTES = 56 * 1024 * 1024


def _rmsnorm(x, g):
    ms = jnp.mean(x * x, axis=-1, keepdims=True)
    return x * lax.rsqrt(ms + EPS) * g


def _sigmoid(x):
    return 1.0 / (1.0 + jnp.exp(-x))


def _resident(shape):
    return pl.BlockSpec(shape, lambda *_: (0,) * len(shape), pipeline_mode=pl.Buffered(1))


FFN_T = 512
FFN_CH = 256
FFN_NCH = D_FF // FFN_CH


def _ffn_kernel(h_ref, g_ref, wup_ref, cw_ref, cb_ref, wdn_ref, fg_ref, o_ref,
                xn_ref, work0_ref, work1_ref, act0_ref, act1_ref, halo_ref, acc_ref, *, final_norm):
    T, CH = FFN_T, FFN_CH

    @pl.when(pl.program_id(1) == 0)
    def _():
        halo_ref[...] = jnp.zeros(halo_ref.shape, F32)

    h = h_ref[0]
    xn_ref[...] = _rmsnorm(h, g_ref[...]).astype(BF16)
    acc_ref[...] = h

    works = (work0_ref, work1_ref)
    acts = (act0_ref, act1_ref)

    def up_proj(j):
        work_ref = works[j % 2]
        work_ref[0:HALO, :] = halo_ref[j]
        work_ref[HALO:HALO + T, :] = jnp.dot(xn_ref[...], wup_ref[j], preferred_element_type=F32)
        halo_ref[j] = work_ref[T:T + HALO, :]

    def down_proj(j):
        acc_ref[...] += jnp.dot(acts[j % 2][...], wdn_ref[j], preferred_element_type=F32)

    up_proj(0)
    for j in range(FFN_NCH):
        if j + 1 < FFN_NCH:
            up_proj(j + 1)
        if j >= 1:
            down_proj(j - 1)
        work_ref = works[j % 2]
        cw = cw_ref[j]
        y = cb_ref[j]
        for k in range(FFN_TAPS):
            off = HALO - (FFN_TAPS - 1) + k
            y = y + work_ref[off:off + T, :] * cw[k:k + 1, :]
        gate = y[:, :CH]
        val = y[:, CH:]
        acts[j % 2][...] = (gate * _sigmoid(gate) * val).astype(BF16)
    down_proj(FFN_NCH - 1)

    out = acc_ref[...]
    if final_norm:
        out = _rmsnorm(out, fg_ref[...])
    o_ref[0] = out


def _ffn(h, g, w_up, conv_w, conv_b, w_down, final_g, *, final_norm):
    B, S, D = h.shape
    T, CH, NCH = FFN_T, FFN_CH, FFN_NCH
    wup = w_up.reshape(D, 2, NCH, CH).transpose(2, 0, 1, 3).reshape(NCH, D, 2 * CH).astype(BF16)
    cw = conv_w.reshape(FFN_TAPS, 2, NCH, CH).transpose(2, 0, 1, 3).reshape(NCH, FFN_TAPS, 2 * CH)
    cb = conv_b.reshape(1, 2, NCH, CH).transpose(2, 0, 1, 3).reshape(NCH, 1, 2 * CH)
    wdn = w_down.reshape(NCH, CH, D).astype(BF16)
    tile = pl.BlockSpec((1, T, D), lambda b, t: (b, t, 0))
    return pl.pallas_call(
        functools.partial(_ffn_kernel, final_norm=final_norm),
        out_shape=jax.ShapeDtypeStruct((B, S, D), F32),
        grid=(B, S // T),
        in_specs=[tile, _resident((1, D)), _resident((NCH, D, 2 * CH)), _resident((NCH, FFN_TAPS, 2 * CH)),
                  _resident((NCH, 1, 2 * CH)), _resident((NCH, CH, D)), _resident((1, D))],
        out_specs=tile,
        scratch_shapes=[pltpu.VMEM((T, D), BF16),
                        pltpu.VMEM((T + HALO, 2 * CH), F32),
                        pltpu.VMEM((T + HALO, 2 * CH), F32),
                        pltpu.VMEM((T, CH), BF16),
                        pltpu.VMEM((T, CH), BF16),
                        pltpu.VMEM((NCH, HALO, 2 * CH), F32),
                        pltpu.VMEM((T, D), F32)],
        compiler_params=pltpu.CompilerParams(dimension_semantics=("arbitrary", "arbitrary"),
                                             vmem_limit_bytes=VMEM_LIMIT_BYTES),
        name="ffn_final" if final_norm else "ffn",
    )(h, g.reshape(1, D), wup, cw, cb, wdn, final_g.reshape(1, D))


EV_T = 512
MIX_W = 512
LRU_TAPS = 4
SC_TAPS = 3
LRU_DECAY_C = 8.0
GELU_C = 0.7978845608028654


def _gelu(x):
    return x * (0.5 * (1.0 + jnp.tanh(GELU_C * (x + 0.044715 * (x * x * x)))))


def _causal_taps(buf_ref, w_ref, b_ref, taps, T):
    w = w_ref[...]
    y = b_ref[...]
    for k in range(taps):
        off = HALO - (taps - 1) + k
        y = y + buf_ref[off:off + T, :] * w[k:k + 1, :]
    return y


def _even_kernel(h_ref, g_ref, win_ref, lcw_ref, lcb_ref, wg_ref, bg_ref, lam_ref, scw_ref, scb_ref, wout_ref,
                 o_ref, xn_ref, xa_ref, cv_ref, a_ref, hs_ref, y_ref, state_ref):
    T, W = EV_T, MIX_W

    @pl.when(pl.program_id(1) == 0)
    def _():
        xa_ref[0:HALO, :] = jnp.zeros((HALO, W), F32)
        cv_ref[0:HALO, :] = jnp.zeros((HALO, W), F32)
        state_ref[...] = jnp.zeros(state_ref.shape, F32)

    h = h_ref[0]
    xn_ref[...] = _rmsnorm(h, g_ref[...]).astype(BF16)

    def proj(c):
        return jnp.dot(xn_ref[...], win_ref[:, c * W:(c + 1) * W], preferred_element_type=F32)

    xa_ref[HALO:HALO + T, :] = proj(0)
    xc = _causal_taps(xa_ref, lcw_ref, lcb_ref, LRU_TAPS, T)
    xa_ref[0:HALO, :] = xa_ref[T:T + HALO, :]
    pre = jnp.dot(xc.astype(BF16), wg_ref[...], preferred_element_type=F32) + bg_ref[...]
    r = _sigmoid(pre[:, :W])
    i = _sigmoid(pre[:, W:])
    lam = lam_ref[...]
    softplus_neg_lam = jnp.maximum(-lam, 0.0) + jnp.log1p(jnp.exp(-jnp.abs(lam)))
    log_a = (-LRU_DECAY_C * r) * softplus_neg_lam
    a = jnp.exp(log_a)
    th = jnp.tanh(log_a)
    u = jnp.sqrt((-2.0 * th) / (1.0 - th)) * (i * xc)

    G = T // 8
    a3 = a.reshape(G, 8, W)
    u3 = u.reshape(G, 8, W)
    row = lax.broadcasted_iota(jnp.int32, (G, 8, W), 1)
    for d in (1, 2, 4):
        keep = row >= d
        u3 = u3 + a3 * jnp.where(keep, pltpu.roll(u3, d, axis=1), 0.0)
        a3 = a3 * jnp.where(keep, pltpu.roll(a3, d, axis=1), 1.0)
    a_ref[...] = a3.reshape(T, W)
    hs_ref[...] = u3.reshape(T, W)
    carry = state_ref[0:1, :]
    for gi in range(G):
        rows = slice(gi * 8, gi * 8 + 8)
        hg = hs_ref[rows, :] + a_ref[rows, :] * carry
        hs_ref[rows, :] = hg
        carry = hg[7:8, :]
    state_ref[0:1, :] = carry
    y_ref[:, 0:W] = (hs_ref[...] * _gelu(proj(1))).astype(BF16)

    cv_ref[HALO:HALO + T, :] = proj(2) * proj(4)
    cc = _causal_taps(cv_ref, scw_ref, scb_ref, SC_TAPS, T)
    cv_ref[0:HALO, :] = cv_ref[T:T + HALO, :]
    y_ref[:, W:2 * W] = (proj(3) * cc).astype(BF16)

    o_ref[0] = h + jnp.dot(y_ref[...], wout_ref[...], preferred_element_type=F32)


def _block_diag(w):
    H, d, _ = w.shape
    return jnp.einsum('hij,hg->higj', w, jnp.eye(H, dtype=w.dtype)).reshape(H * d, H * d)


def _even_mixer(h, g, w_in, lcw, lcb, wa, ba, wx, bx, lam, scw, scb, w_out):
    B, S, D = h.shape
    T, W = EV_T, MIX_W
    wg = jnp.concatenate([_block_diag(wa), _block_diag(wx)], axis=1).astype(BF16)
    bg = jnp.concatenate([ba, bx]).reshape(1, 2 * W)
    tile = pl.BlockSpec((1, T, D), lambda b, t: (b, t, 0))
    return pl.pallas_call(
        _even_kernel,
        out_shape=jax.ShapeDtypeStruct((B, S, D), F32),
        grid=(B, S // T),
        in_specs=[tile, _resident((1, D)), _resident((D, 5 * W)), _resident((LRU_TAPS, W)), _resident((1, W)),
                  _resident((W, 2 * W)), _resident((1, 2 * W)), _resident((1, W)), _resident((SC_TAPS, W)),
                  _resident((1, W)), _resident((2 * W, D))],
        out_specs=tile,
        scratch_shapes=[pltpu.VMEM((T, D), BF16),
                        pltpu.VMEM((T + HALO, W), F32),
                        pltpu.VMEM((T + HALO, W), F32),
                        pltpu.VMEM((T, W), F32),
                        pltpu.VMEM((T, W), F32),
                        pltpu.VMEM((T, 2 * W), BF16),
                        pltpu.VMEM((8, W), F32)],
        compiler_params=pltpu.CompilerParams(dimension_semantics=("arbitrary", "arbitrary"),
                                             vmem_limit_bytes=VMEM_LIMIT_BYTES),
        name="even_mixer",
    )(h, g.reshape(1, D), w_in.astype(BF16), lcw, lcb.reshape(1, W), wg, bg, lam.reshape(1, W), scw,
      scb.reshape(1, W), w_out.astype(BF16))


OD_T = 512
SGU_CHUNK = 128
N_HEADS = 8
HEAD_DIM = 64
F_PAD = 128
HEADS_PER_HALF = 4
QK_SCALE = HEAD_DIM ** -0.5


def _split3_bf16(x):
    hi = x.astype(BF16)
    r1 = x - hi.astype(F32)
    mid = r1.astype(BF16)
    lo = (r1 - mid.astype(F32)).astype(BF16)
    return hi, mid, lo


def _odd_in_kernel(h_ref, g_ref, win_ref, gn_ref, seg_ref, sw_ref, sb_ref, bf_ref,
                   yc_ref, q_ref, k_ref, v_ref, c_ref, xn_ref, gv_ref, u_ref, csum_ref):
    T, W = OD_T, MIX_W

    @pl.when(pl.program_id(1) == 0)
    def _():
        csum_ref[...] = jnp.zeros(csum_ref.shape, F32)

    xn_ref[...] = _rmsnorm(h_ref[0], g_ref[...]).astype(BF16)

    def proj(lo, width):
        return jnp.dot(xn_ref[...], win_ref[:, lo:lo + width], preferred_element_type=F32)

    q_ref[0] = (proj(2 * W, W) * QK_SCALE).astype(BF16)
    k_ref[0] = proj(3 * W, W).astype(BF16)
    v_ref[0] = proj(4 * W, W).astype(BF16)

    fl = proj(5 * W, F_PAD) + bf_ref[...]
    logf = jnp.minimum(fl, 0.0) - jnp.log1p(jnp.exp(-jnp.abs(fl)))
    logf_t = logf.T[0:16, :]
    tri = (lax.broadcasted_iota(jnp.int32, (T, T), 0) <= lax.broadcasted_iota(jnp.int32, (T, T), 1)).astype(BF16)
    c_loc = sum(jnp.dot(part, tri, preferred_element_type=F32) for part in _split3_bf16(logf_t))
    c_tile = c_loc[0:N_HEADS, :] + csum_ref[:, 0:1]
    c_ref[0] = c_tile
    csum_ref[...] = jnp.broadcast_to(c_tile[:, T - 1:T], csum_ref.shape)

    u_ref[...] = _gelu(proj(0, W))
    g = _gelu(proj(W, W))
    gsq = g * g
    gsq_hi = gsq.astype(BF16)
    gsq_lo = (gsq - gsq_hi.astype(F32)).astype(BF16)
    ms = (jnp.dot(gsq_hi, seg_ref[...], preferred_element_type=F32)
          + jnp.dot(gsq_lo, seg_ref[...], preferred_element_type=F32)) * (1.0 / HEAD_DIM)
    gv_ref[...] = (g * lax.rsqrt(ms + EPS) * gn_ref[...]).astype(BF16)

    HW = HEADS_PER_HALF * HEAD_DIM
    lane_head = lax.broadcasted_iota(jnp.int32, (SGU_CHUNK, HW), 1) // HEAD_DIM
    wrow = lax.broadcasted_iota(jnp.int32, (SGU_CHUNK, HEADS_PER_HALF * SGU_CHUNK), 0)
    wcol = lax.broadcasted_iota(jnp.int32, (SGU_CHUNK, HEADS_PER_HALF * SGU_CHUNK), 1) % SGU_CHUNK
    for half in range(2):
        cols = slice(half * HW, (half + 1) * HW)
        w_causal = jnp.where(wcol <= wrow, sw_ref[half], 0.0).astype(BF16)
        bias = sb_ref[:, cols]
        for c in range(T // SGU_CHUNK):
            rows = slice(c * SGU_CHUNK, (c + 1) * SGU_CHUNK)
            gvc = gv_ref[rows, cols]
            stacked = jnp.concatenate(
                [jnp.where(lane_head == j, gvc, jnp.zeros_like(gvc)) for j in range(HEADS_PER_HALF)], axis=0)
            mixed = jnp.dot(w_causal, stacked, preferred_element_type=F32) + bias
            yc_ref[0, rows, cols] = (u_ref[rows, cols] * mixed).astype(BF16)


def _odd_in(h, g, w_in, sgu_norm, sgu_w, sgu_b, fox_bf):
    B, S, D = h.shape
    T, W = OD_T, MIX_W
    n_in = 5 * W + F_PAD
    win = jnp.pad(w_in, ((0, 0), (0, n_in - w_in.shape[1]))).astype(BF16)
    seg = _block_diag(jnp.ones((N_HEADS, HEAD_DIM, HEAD_DIM), F32)).astype(BF16)
    sw = sgu_w.reshape(2, HEADS_PER_HALF, SGU_CHUNK, SGU_CHUNK).transpose(0, 2, 1, 3).reshape(
        2, SGU_CHUNK, HEADS_PER_HALF * SGU_CHUNK)
    sb = jnp.repeat(sgu_b.T, HEAD_DIM, axis=1)
    bf = jnp.pad(fox_bf, (0, F_PAD - N_HEADS)).reshape(1, F_PAD)
    tile = pl.BlockSpec((1, T, D), lambda b, t: (b, t, 0))
    half_tile = pl.BlockSpec((1, T, W), lambda b, t: (b, t, 0))
    act = jax.ShapeDtypeStruct((B, S, W), BF16)
    return pl.pallas_call(
        _odd_in_kernel,
        out_shape=(act, act, act, act, jax.ShapeDtypeStruct((B, N_HEADS, S), F32)),
        grid=(B, S // T),
        in_specs=[tile, _resident((1, D)), _resident((D, n_in)), _resident((1, W)), _resident((W, W)),
                  _resident((2, SGU_CHUNK, HEADS_PER_HALF * SGU_CHUNK)), _resident((SGU_CHUNK, W)),
                  _resident((1, F_PAD))],
        out_specs=(half_tile, half_tile, half_tile, half_tile,
                   pl.BlockSpec((1, N_HEADS, T), lambda b, t: (b, 0, t))),
        scratch_shapes=[pltpu.VMEM((T, D), BF16),
                        pltpu.VMEM((T, W), BF16),
                        pltpu.VMEM((T, W), F32),
                        pltpu.VMEM((N_HEADS, 128), F32)],
        compiler_params=pltpu.CompilerParams(dimension_semantics=("arbitrary", "arbitrary"),
                                             vmem_limit_bytes=VMEM_LIMIT_BYTES),
        name="odd_in",
    )(h, g.reshape(1, D), win, sgu_norm.reshape(1, W), seg, sw, sb, bf)


AT_T = 256
PAIR_W = 2 * HEAD_DIM
N_PAIRS = N_HEADS // 2
M_INIT = -1e30


def _attn_kernel(q_ref, k_ref, v_ref, c_ref, yc_ref, h_ref, wout_ref, o_ref,
                 qm_ref, m_ref, l_ref, acc_ref, yd_ref):
    T, W = AT_T, MIX_W
    qi = pl.program_id(1)
    q0 = pl.multiple_of(qi * T, T)
    low = lax.broadcasted_iota(jnp.int32, (T, PAIR_W), 1) < HEAD_DIM

    for pair in range(N_PAIRS):
        q2 = q_ref[0, :, pair * PAIR_W:(pair + 1) * PAIR_W]
        zero = jnp.zeros_like(q2)
        qm_ref[2 * pair] = jnp.where(low, q2, zero)
        qm_ref[2 * pair + 1] = jnp.where(low, zero, q2)
    m_ref[...] = jnp.full(m_ref.shape, M_INIT, F32)
    l_ref[...] = jnp.zeros(l_ref.shape, F32)
    acc_ref[...] = jnp.zeros(acc_ref.shape, F32)

    c_q0 = c_ref[0, :, pl.ds(q0, 128)][:, 0:1]

    def block(j, diagonal):
        k0 = pl.multiple_of(j * T, T)
        cb = c_q0 - c_ref[0, :, pl.ds(k0, T)]
        if diagonal:
            visible = (lax.broadcasted_iota(jnp.int32, (T, T), 1) <= lax.broadcasted_iota(jnp.int32, (T, T), 0))
        for pair in range(N_PAIRS):
            cols = slice(pair * PAIR_W, (pair + 1) * PAIR_W)
            k2 = k_ref[0, pl.ds(k0, T), cols]
            v2 = v_ref[0, pl.ds(k0, T), cols]
            upd = []
            for half in range(2):
                hh = 2 * pair + half
                s = lax.dot_general(qm_ref[hh], k2, (((1,), (1,)), ((), ())), preferred_element_type=F32)
                s = s + cb[hh:hh + 1, :]
                if diagonal:
                    s = jnp.where(visible, s, -jnp.inf)
                m_old = m_ref[hh]
                m_new = jnp.maximum(m_old, jnp.max(s, axis=-1, keepdims=True))
                alpha = jnp.exp(m_old - m_new)
                p = jnp.exp(s - jnp.concatenate([m_new] * (T // PAIR_W), axis=1))
                l_ref[hh] = alpha * l_ref[hh] + jnp.sum(p, axis=-1, keepdims=True)
                m_ref[hh] = m_new
                upd.append(alpha * acc_ref[pair] + jnp.dot(p.astype(BF16), v2, preferred_element_type=F32))
            acc_ref[pair] = jnp.where(low, upd[0], upd[1])

    def off_diagonal(j, carry):
        block(j, False)
        return carry

    lax.fori_loop(0, qi, off_diagonal, 0)
    block(qi, True)

    for pair in range(N_PAIRS):
        inv_l = jnp.where(low, 1.0 / l_ref[2 * pair], 1.0 / l_ref[2 * pair + 1])
        yd_ref[:, pair * PAIR_W:(pair + 1) * PAIR_W] = (acc_ref[pair] * inv_l).astype(BF16)
    o_ref[0] = (h_ref[0]
                + jnp.dot(yc_ref[0], wout_ref[0:W, :], preferred_element_type=F32)
                + jnp.dot(yd_ref[...], wout_ref[W:2 * W, :], preferred_element_type=F32))


def _attention(q, k, v, c, yc, h, w_out):
    B, S, D = h.shape
    T, W = AT_T, MIX_W
    q_tile = pl.BlockSpec((1, T, W), lambda b, t: (b, t, 0))
    seq = pl.BlockSpec((1, S, W), lambda b, t: (b, 0, 0))
    tile = pl.BlockSpec((1, T, D), lambda b, t: (b, t, 0))
    return pl.pallas_call(
        _attn_kernel,
        out_shape=jax.ShapeDtypeStruct((B, S, D), F32),
        grid=(B, S // T),
        in_specs=[q_tile, seq, seq, pl.BlockSpec((1, N_HEADS, S), lambda b, t: (b, 0, 0)), q_tile, tile,
                  _resident((2 * W, D))],
        out_specs=tile,
        scratch_shapes=[pltpu.VMEM((N_HEADS, T, PAIR_W), BF16),
                        pltpu.VMEM((N_HEADS, T, PAIR_W), F32),
                        pltpu.VMEM((N_HEADS, T, PAIR_W), F32),
                        pltpu.VMEM((N_PAIRS, T, PAIR_W), F32),
                        pltpu.VMEM((T, W), BF16)],
        compiler_params=pltpu.CompilerParams(dimension_semantics=("arbitrary", "arbitrary"),
                                             vmem_limit_bytes=VMEM_LIMIT_BYTES),
        name="fox_attention",
    )(q, k, v, c, yc, h, w_out.astype(BF16))


def kernel(x, mix0_norm, mix0_w_in, lru_conv_w, lru_conv_b, lru_wa, lru_ba, lru_wx, lru_bx, lru_lambda,
           sconv_w, sconv_b, mix0_w_out, mix1_norm, mix1_w_in, sgu_norm, sgu_w, sgu_b, fox_bf, mix1_w_out,
           ffn_norm, ffn_up, ffn_conv_w, ffn_conv_b, ffn_down, final_norm):
    h = x
    h = _even_mixer(h, mix0_norm[0], mix0_w_in[0], lru_conv_w[0], lru_conv_b[0], lru_wa[0], lru_ba[0],
                    lru_wx[0], lru_bx[0], lru_lambda[0], sconv_w[0], sconv_b[0], mix0_w_out[0])
    h = _ffn(h, ffn_norm[0], ffn_up[0], ffn_conv_w[0], ffn_conv_b[0], ffn_down[0], final_norm, final_norm=False)
    yc, q, k, v, c = _odd_in(h, mix1_norm[0], mix1_w_in[0], sgu_norm[0], sgu_w[0], sgu_b[0], fox_bf[0])
    h = _attention(q, k, v, c, yc, h, mix1_w_out[0])
    h = _ffn(h, ffn_norm[1], ffn_up[1], ffn_conv_w[1], ffn_conv_b[1], ffn_down[1], final_norm, final_norm=True)
    return h
```

```python
import functools

import jax
import jax.numpy as jnp
from jax import lax
from jax.experimental import pallas as pl
from jax.experimental.pallas import tpu as pltpu

F32 = jnp.float32
BF16 = jnp.bfloat16

EPS = 1e-6
D_MODEL = 1024
D_FF = 2816
FFN_TAPS = 3
HALO = 8
VMEM_LIMIT_BYTES = 56 * 1024 * 1024


def _rmsnorm(x, g):
    ms = jnp.mean(x * x, axis=-1, keepdims=True)
    return x * lax.rsqrt(ms + EPS) * g


def _sigmoid(x):
    return 1.0 / (1.0 + jnp.exp(-x))


def _resident(shape):
    return pl.BlockSpec(shape, lambda *_: (0,) * len(shape), pipeline_mode=pl.Buffered(1))


FFN_T = 512
FFN_CH = 256
FFN_NCH = D_FF // FFN_CH


def _ffn_kernel(h_ref, g_ref, wup_ref, cw_ref, cb_ref, wdn_ref, fg_ref, o_ref,
                xn_ref, work0_ref, work1_ref, act0_ref, act1_ref, halo_ref, acc_ref, *, final_norm):
    T, CH = FFN_T, FFN_CH

    @pl.when(pl.program_id(1) == 0)
    def _():
        halo_ref[...] = jnp.zeros(halo_ref.shape, F32)

    h = h_ref[0]
    xn_ref[...] = _rmsnorm(h, g_ref[...]).astype(BF16)
    acc_ref[...] = h

    works = (work0_ref, work1_ref)
    acts = (act0_ref, act1_ref)

    def up_proj(j):
        work_ref = works[j % 2]
        work_ref[0:HALO, :] = halo_ref[j]
        work_ref[HALO:HALO + T, :] = jnp.dot(xn_ref[...], wup_ref[j], preferred_element_type=F32)
        halo_ref[j] = work_ref[T:T + HALO, :]

    def down_proj(j):
        acc_ref[...] += jnp.dot(acts[j % 2][...], wdn_ref[j], preferred_element_type=F32)

    up_proj(0)
    for j in range(FFN_NCH):
        if j + 1 < FFN_NCH:
            up_proj(j + 1)
        if j >= 1:
            down_proj(j - 1)
        work_ref = works[j % 2]
        cw = cw_ref[j]
        y = cb_ref[j]
        for k in range(FFN_TAPS):
            off = HALO - (FFN_TAPS - 1) + k
            y = y + work_ref[off:off + T, :] * cw[k:k + 1, :]
        gate = y[:, :CH]
        val = y[:, CH:]
        acts[j % 2][...] = (gate * _sigmoid(gate) * val).astype(BF16)
    down_proj(FFN_NCH - 1)

    out = acc_ref[...]
    if final_norm:
        out = _rmsnorm(out, fg_ref[...])
    o_ref[0] = out


def _ffn(h, g, w_up, conv_w, conv_b, w_down, final_g, *, final_norm):
    B, S, D = h.shape
    T, CH, NCH = FFN_T, FFN_CH, FFN_NCH
    wup = w_up.reshape(D, 2, NCH, CH).transpose(2, 0, 1, 3).reshape(NCH, D, 2 * CH).astype(BF16)
    cw = conv_w.reshape(FFN_TAPS, 2, NCH, CH).transpose(2, 0, 1, 3).reshape(NCH, FFN_TAPS, 2 * CH)
    cb = conv_b.reshape(1, 2, NCH, CH).transpose(2, 0, 1, 3).reshape(NCH, 1, 2 * CH)
    wdn = w_down.reshape(NCH, CH, D).astype(BF16)
    tile = pl.BlockSpec((1, T, D), lambda b, t: (b, t, 0))
    return pl.pallas_call(
        functools.partial(_ffn_kernel, final_norm=final_norm),
        out_shape=jax.ShapeDtypeStruct((B, S, D), F32),
        grid=(B, S // T),
        in_specs=[tile, _resident((1, D)), _resident((NCH, D, 2 * CH)), _resident((NCH, FFN_TAPS, 2 * CH)),
                  _resident((NCH, 1, 2 * CH)), _resident((NCH, CH, D)), _resident((1, D))],
        out_specs=tile,
        scratch_shapes=[pltpu.VMEM((T, D), BF16),
                        pltpu.VMEM((T + HALO, 2 * CH), F32),
                        pltpu.VMEM((T + HALO, 2 * CH), F32),
                        pltpu.VMEM((T, CH), BF16),
                        pltpu.VMEM((T, CH), BF16),
                        pltpu.VMEM((NCH, HALO, 2 * CH), F32),
                        pltpu.VMEM((T, D), F32)],
        compiler_params=pltpu.CompilerParams(dimension_semantics=("arbitrary", "arbitrary"),
                                             vmem_limit_bytes=VMEM_LIMIT_BYTES),
        name="ffn_final" if final_norm else "ffn",
    )(h, g.reshape(1, D), wup, cw, cb, wdn, final_g.reshape(1, D))


EV_T = 512
MIX_W = 512
LRU_TAPS = 4
SC_TAPS = 3
LRU_DECAY_C = 8.0
GELU_C = 0.7978845608028654


def _gelu(x):
    return x * (0.5 * (1.0 + jnp.tanh(GELU_C * (x + 0.044715 * (x * x * x)))))


def _causal_taps(buf_ref, w_ref, b_ref, taps, T):
    w = w_ref[...]
    y = b_ref[...]
    for k in range(taps):
        off = HALO - (taps - 1) + k
        y = y + buf_ref[off:off + T, :] * w[k:k + 1, :]
    return y


def _even_kernel(h_ref, g_ref, win_ref, lcw_ref, lcb_ref, wg_ref, bg_ref, lam_ref, scw_ref, scb_ref, wout_ref,
                 o_ref, xn_ref, xa_ref, cv_ref, a_ref, hs_ref, y_ref, state_ref):
    T, W = EV_T, MIX_W

    @pl.when(pl.program_id(1) == 0)
    def _():
        xa_ref[0:HALO, :] = jnp.zeros((HALO, W), F32)
        cv_ref[0:HALO, :] = jnp.zeros((HALO, W), F32)
        state_ref[...] = jnp.zeros(state_ref.shape, F32)

    h = h_ref[0]
    xn_ref[...] = _rmsnorm(h, g_ref[...]).astype(BF16)

    def proj(c):
        return jnp.dot(xn_ref[...], win_ref[:, c * W:(c + 1) * W], preferred_element_type=F32)

    xa_ref[HALO:HALO + T, :] = proj(0)
    xc = _causal_taps(xa_ref, lcw_ref, lcb_ref, LRU_TAPS, T)
    xa_ref[0:HALO, :] = xa_ref[T:T + HALO, :]
    pre = jnp.dot(xc.astype(BF16), wg_ref[...], preferred_element_type=F32) + bg_ref[...]
    r = _sigmoid(pre[:, :W])
    i = _sigmoid(pre[:, W:])
    lam = lam_ref[...]
    softplus_neg_lam = jnp.maximum(-lam, 0.0) + jnp.log1p(jnp.exp(-jnp.abs(lam)))
    log_a = (-LRU_DECAY_C * r) * softplus_neg_lam
    a = jnp.exp(log_a)
    th = jnp.tanh(log_a)
    u = jnp.sqrt((-2.0 * th) / (1.0 - th)) * (i * xc)

    G = T // 8
    a3 = a.reshape(G, 8, W)
    u3 = u.reshape(G, 8, W)
    row = lax.broadcasted_iota(jnp.int32, (G, 8, W), 1)
    for d in (1, 2, 4):
        keep = row >= d
        u3 = u3 + a3 * jnp.where(keep, pltpu.roll(u3, d, axis=1), 0.0)
        a3 = a3 * jnp.where(keep, pltpu.roll(a3, d, axis=1), 1.0)
    a_ref[...] = a3.reshape(T, W)
    hs_ref[...] = u3.reshape(T, W)
    carry = state_ref[0:1, :]
    for gi in range(G):
        rows = slice(gi * 8, gi * 8 + 8)
        hg = hs_ref[rows, :] + a_ref[rows, :] * carry
        hs_ref[rows, :] = hg
        carry = hg[7:8, :]
    state_ref[0:1, :] = carry
    y_ref[:, 0:W] = (hs_ref[...] * _gelu(proj(1))).astype(BF16)

    cv_ref[HALO:HALO + T, :] = proj(2) * proj(4)
    cc = _causal_taps(cv_ref, scw_ref, scb_ref, SC_TAPS, T)
    cv_ref[0:HALO, :] = cv_ref[T:T + HALO, :]
    y_ref[:, W:2 * W] = (proj(3) * cc).astype(BF16)

    o_ref[0] = h + jnp.dot(y_ref[...], wout_ref[...], preferred_element_type=F32)


def _block_diag(w):
    H, d, _ = w.shape
    return jnp.einsum('hij,hg->higj', w, jnp.eye(H, dtype=w.dtype)).reshape(H * d, H * d)


def _even_mixer(h, g, w_in, lcw, lcb, wa, ba, wx, bx, lam, scw, scb, w_out):
    B, S, D = h.shape
    T, W = EV_T, MIX_W
    wg = jnp.concatenate([_block_diag(wa), _block_diag(wx)], axis=1).astype(BF16)
    bg = jnp.concatenate([ba, bx]).reshape(1, 2 * W)
    tile = pl.BlockSpec((1, T, D), lambda b, t: (b, t, 0))
    return pl.pallas_call(
        _even_kernel,
        out_shape=jax.ShapeDtypeStruct((B, S, D), F32),
        grid=(B, S // T),
        in_specs=[tile, _resident((1, D)), _resident((D, 5 * W)), _resident((LRU_TAPS, W)), _resident((1, W)),
                  _resident((W, 2 * W)), _resident((1, 2 * W)), _resident((1, W)), _resident((SC_TAPS, W)),
                  _resident((1, W)), _resident((2 * W, D))],
        out_specs=tile,
        scratch_shapes=[pltpu.VMEM((T, D), BF16),
                        pltpu.VMEM((T + HALO, W), F32),
                        pltpu.VMEM((T + HALO, W), F32),
                        pltpu.VMEM((T, W), F32),
                        pltpu.VMEM((T, W), F32),
                        pltpu.VMEM((T, 2 * W), BF16),
                        pltpu.VMEM((8, W), F32)],
        compiler_params=pltpu.CompilerParams(dimension_semantics=("arbitrary", "arbitrary"),
                                             vmem_limit_bytes=VMEM_LIMIT_BYTES),
        name="even_mixer",
    )(h, g.reshape(1, D), w_in.astype(BF16), lcw, lcb.reshape(1, W), wg, bg, lam.reshape(1, W), scw,
      scb.reshape(1, W), w_out.astype(BF16))


OD_T = 512
SGU_CHUNK = 128
N_HEADS = 8
HEAD_DIM = 64
F_PAD = 128
HEADS_PER_HALF = 4
QK_SCALE = HEAD_DIM ** -0.5
LOG2E = 1.4426950408889634
PAIR_W = 2 * HEAD_DIM
N_PAIRS = N_HEADS // 2


def _split3_bf16(x):
    hi = x.astype(BF16)
    r1 = x - hi.astype(F32)
    mid = r1.astype(BF16)
    lo = (r1 - mid.astype(F32)).astype(BF16)
    return hi, mid, lo


def _odd_in_kernel(h_ref, g_ref, win_ref, gn_ref, seg_ref, sw_ref, sb_ref, bf_ref,
                   yc_ref, q_ref, k_ref, v_ref, xn_ref, gv_ref, u_ref, csum_ref):
    T, W = OD_T, MIX_W

    @pl.when(pl.program_id(1) == 0)
    def _():
        csum_ref[...] = jnp.zeros(csum_ref.shape, F32)

    xn_ref[...] = _rmsnorm(h_ref[0], g_ref[...]).astype(BF16)

    def proj(lo, width):
        return jnp.dot(xn_ref[...], win_ref[:, lo:lo + width], preferred_element_type=F32)

    fl = proj(5 * W, F_PAD) + bf_ref[...]
    logf = jnp.minimum(fl, 0.0) - jnp.log1p(jnp.exp(-jnp.abs(fl)))
    tri = (lax.broadcasted_iota(jnp.int32, (T, T), 1) <= lax.broadcasted_iota(jnp.int32, (T, T), 0)).astype(BF16)
    c_loc = sum(jnp.dot(tri, part, preferred_element_type=F32) for part in _split3_bf16(logf))
    c_tile = c_loc + csum_ref[0:1, :]
    csum_ref[0:1, :] = c_tile[T - 1:T, :]

    lane = lax.broadcasted_iota(jnp.int32, (T, PAIR_W), 1)
    low = lane < HEAD_DIM
    pq = proj(2 * W, W) * (QK_SCALE * LOG2E)
    pk = proj(3 * W, W)
    pv = proj(4 * W, W)
    for hh in range(N_HEADS):
        blk = slice((hh // 2) * PAIR_W, (hh // 2 + 1) * PAIR_W)
        own = low if hh % 2 == 0 else jnp.logical_not(low)
        a0 = HEAD_DIM if hh % 2 == 0 else 0
        x = jnp.broadcast_to(c_tile[:, hh:hh + 1] * (-LOG2E), (T, PAIR_W))
        hi = x.astype(BF16).astype(F32)
        r1 = x - hi
        mid = r1.astype(BF16).astype(F32)
        lo = r1 - mid
        c_parts = jnp.where(lane == a0, hi, jnp.where(lane == a0 + 1, mid, jnp.where(lane == a0 + 2, lo, 0.0)))
        ones3 = jnp.where((lane >= a0) & (lane < a0 + 3), 1.0, 0.0)
        one1 = jnp.where(lane == a0, 1.0, 0.0)
        q_ref[0, hh] = jnp.where(own, pq[:, blk], ones3).astype(BF16)
        k_ref[0, hh] = jnp.where(own, pk[:, blk], c_parts).astype(BF16)
        v_ref[0, hh] = jnp.where(own, pv[:, blk], one1).astype(BF16)

    u_ref[...] = _gelu(proj(0, W))
    g = _gelu(proj(W, W))
    gsq = g * g
    gsq_hi = gsq.astype(BF16)
    gsq_lo = (gsq - gsq_hi.astype(F32)).astype(BF16)
    ms = (jnp.dot(gsq_hi, seg_ref[...], preferred_element_type=F32)
          + jnp.dot(gsq_lo, seg_ref[...], preferred_element_type=F32)) * (1.0 / HEAD_DIM)
    gv_ref[...] = (g * lax.rsqrt(ms + EPS) * gn_ref[...]).astype(BF16)

    HW = HEADS_PER_HALF * HEAD_DIM
    lane_head = lax.broadcasted_iota(jnp.int32, (SGU_CHUNK, HW), 1) // HEAD_DIM
    wrow = lax.broadcasted_iota(jnp.int32, (SGU_CHUNK, HEADS_PER_HALF * SGU_CHUNK), 0)
    wcol = lax.broadcasted_iota(jnp.int32, (SGU_CHUNK, HEADS_PER_HALF * SGU_CHUNK), 1) % SGU_CHUNK
    for half in range(2):
        cols = slice(half * HW, (half + 1) * HW)
        w_causal = jnp.where(wcol <= wrow, sw_ref[half], 0.0).astype(BF16)
        bias = sb_ref[:, cols]
        for c in range(T // SGU_CHUNK):
            rows = slice(c * SGU_CHUNK, (c + 1) * SGU_CHUNK)
            gvc = gv_ref[rows, cols]
            stacked = jnp.concatenate(
                [jnp.where(lane_head == j, gvc, jnp.zeros_like(gvc)) for j in range(HEADS_PER_HALF)], axis=0)
            mixed = jnp.dot(w_causal, stacked, preferred_element_type=F32) + bias
            yc_ref[0, rows, cols] = (u_ref[rows, cols] * mixed).astype(BF16)


def _odd_in(h, g, w_in, sgu_norm, sgu_w, sgu_b, fox_bf):
    B, S, D = h.shape
    T, W = OD_T, MIX_W
    n_in = 5 * W + F_PAD
    win = jnp.pad(w_in, ((0, 0), (0, n_in - w_in.shape[1]))).astype(BF16)
    seg = _block_diag(jnp.ones((N_HEADS, HEAD_DIM, HEAD_DIM), F32)).astype(BF16)
    sw = sgu_w.reshape(2, HEADS_PER_HALF, SGU_CHUNK, SGU_CHUNK).transpose(0, 2, 1, 3).reshape(
        2, SGU_CHUNK, HEADS_PER_HALF * SGU_CHUNK)
    sb = jnp.repeat(sgu_b.T, HEAD_DIM, axis=1)
    bf = jnp.pad(fox_bf, (0, F_PAD - N_HEADS)).reshape(1, F_PAD)
    tile = pl.BlockSpec((1, T, D), lambda b, t: (b, t, 0))
    half_tile = pl.BlockSpec((1, T, W), lambda b, t: (b, t, 0))
    head_tile = pl.BlockSpec((1, N_HEADS, T, PAIR_W), lambda b, t: (b, 0, t, 0))
    act = jax.ShapeDtypeStruct((B, S, W), BF16)
    per_head = jax.ShapeDtypeStruct((B, N_HEADS, S, PAIR_W), BF16)
    return pl.pallas_call(
        _odd_in_kernel,
        out_shape=(act, per_head, per_head, per_head),
        grid=(B, S // T),
        in_specs=[tile, _resident((1, D)), _resident((D, n_in)), _resident((1, W)), _resident((W, W)),
                  _resident((2, SGU_CHUNK, HEADS_PER_HALF * SGU_CHUNK)), _resident((SGU_CHUNK, W)),
                  _resident((1, F_PAD))],
        out_specs=(half_tile, head_tile, head_tile, head_tile),
        scratch_shapes=[pltpu.VMEM((T, D), BF16),
                        pltpu.VMEM((T, W), BF16),
                        pltpu.VMEM((T, W), F32),
                        pltpu.VMEM((N_HEADS, 128), F32)],
        compiler_params=pltpu.CompilerParams(dimension_semantics=("arbitrary", "arbitrary"),
                                             vmem_limit_bytes=VMEM_LIMIT_BYTES),
        name="odd_in",
    )(h, g.reshape(1, D), win, sgu_norm.reshape(1, W), seg, sw, sb, bf)


AT_T = 512
M_INIT = -1e30
QK_AHEAD = 2


def _attn_kernel(q_ref, k_ref, v_ref, yc_ref, h_ref, wout_ref, o_ref, s_ref, mx_ref, m_ref, acc_ref, yd_ref):
    T, W = AT_T, MIX_W
    qi = pl.program_id(1)
    m_ref[...] = jnp.full(m_ref.shape, M_INIT, F32)
    acc_ref[...] = jnp.zeros(acc_ref.shape, F32)

    def block(j, diagonal):
        k0 = pl.multiple_of(j * T, T)
        if diagonal:
            visible = (lax.broadcasted_iota(jnp.int32, (T, T), 1) <= lax.broadcasted_iota(jnp.int32, (T, T), 0))
        def logits(hh):
            s = lax.dot_general(q_ref[0, hh], k_ref[0, hh, pl.ds(k0, T), :], (((1,), (1,)), ((), ())),
                                preferred_element_type=F32)
            if diagonal:
                s = jnp.where(visible, s, -jnp.inf)
            s_ref[hh] = s
            mx_ref[hh] = jnp.broadcast_to(jnp.max(s, axis=-1, keepdims=True), (T, PAIR_W))

        for hh in range(QK_AHEAD):
            logits(hh)
        for hh in range(N_HEADS):
            if hh + QK_AHEAD < N_HEADS:
                logits(hh + QK_AHEAD)
            m_old = m_ref[hh]
            m_new = jnp.maximum(m_old, mx_ref[hh])
            alpha = jnp.exp2(m_old - m_new)
            p = jnp.exp2(s_ref[hh] - jnp.concatenate([m_new] * (T // PAIR_W), axis=1))
            m_ref[hh] = m_new
            acc_ref[hh] = alpha * acc_ref[hh] + jnp.dot(p.astype(BF16), v_ref[0, hh, pl.ds(k0, T), :],
                                                        preferred_element_type=F32)

    def off_diagonal(j, carry):
        block(j, False)
        return carry

    lax.fori_loop(0, qi, off_diagonal, 0)
    block(qi, True)

    low = lax.broadcasted_iota(jnp.int32, (T, PAIR_W), 1) < HEAD_DIM
    for pair in range(N_PAIRS):
        acc_a = acc_ref[2 * pair]
        acc_b = acc_ref[2 * pair + 1]
        y = jnp.where(low, acc_a / acc_a[:, HEAD_DIM:HEAD_DIM + 1], acc_b / acc_b[:, 0:1])
        yd_ref[:, pair * PAIR_W:(pair + 1) * PAIR_W] = y.astype(BF16)
    o_ref[0] = (h_ref[0]
                + jnp.dot(yc_ref[0], wout_ref[0:W, :], preferred_element_type=F32)
                + jnp.dot(yd_ref[...], wout_ref[W:2 * W, :], preferred_element_type=F32))


def _attention(q, k, v, yc, h, w_out):
    B, S, D = h.shape
    T, W = AT_T, MIX_W
    q_tile = pl.BlockSpec((1, N_HEADS, T, PAIR_W), lambda b, t: (b, 0, t, 0))
    seq = pl.BlockSpec((1, N_HEADS, S, PAIR_W), lambda b, t: (b, 0, 0, 0), pipeline_mode=pl.Buffered(1))
    tile = pl.BlockSpec((1, T, D), lambda b, t: (b, t, 0))
    return pl.pallas_call(
        _attn_kernel,
        out_shape=jax.ShapeDtypeStruct((B, S, D), F32),
        grid=(B, S // T),
        in_specs=[q_tile, seq, seq, pl.BlockSpec((1, T, W), lambda b, t: (b, t, 0)), tile, _resident((2 * W, D))],
        out_specs=tile,
        scratch_shapes=[pltpu.VMEM((N_HEADS, T, T), F32),
                        pltpu.VMEM((N_HEADS, T, PAIR_W), F32),
                        pltpu.VMEM((N_HEADS, T, PAIR_W), F32),
                        pltpu.VMEM((N_HEADS, T, PAIR_W), F32),
                        pltpu.VMEM((T, W), BF16)],
        compiler_params=pltpu.CompilerParams(dimension_semantics=("arbitrary", "arbitrary"),
                                             vmem_limit_bytes=VMEM_LIMIT_BYTES),
        name="fox_attention",
    )(q, k, v, yc, h, w_out.astype(BF16))


def kernel(x, mix0_norm, mix0_w_in, lru_conv_w, lru_conv_b, lru_wa, lru_ba, lru_wx, lru_bx, lru_lambda,
           sconv_w, sconv_b, mix0_w_out, mix1_norm, mix1_w_in, sgu_norm, sgu_w, sgu_b, fox_bf, mix1_w_out,
           ffn_norm, ffn_up, ffn_conv_w, ffn_conv_b, ffn_down, final_norm):
    h = x
    h = _even_mixer(h, mix0_norm[0], mix0_w_in[0], lru_conv_w[0], lru_conv_b[0], lru_wa[0], lru_ba[0],
                    lru_wx[0], lru_bx[0], lru_lambda[0], sconv_w[0], sconv_b[0], mix0_w_out[0])
    h = _ffn(h, ffn_norm[0], ffn_up[0], ffn_conv_w[0], ffn_conv_b[0], ffn_down[0], final_norm, final_norm=False)
    yc, q, k, v = _odd_in(h, mix1_norm[0], mix1_w_in[0], sgu_norm[0], sgu_w[0], sgu_b[0], fox_bf[0])
    h = _attention(q, k, v, yc, h, mix1_w_out[0])
    h = _ffn(h, ffn_norm[1], ffn_up[1], ffn_conv_w[1], ffn_conv_b[1], ffn_down[1], final_norm, final_norm=True)
    return h
```

```python
import functools

import jax
import jax.numpy as jnp
from jax import lax
from jax.experimental import pallas as pl
from jax.experimental.pallas import tpu as pltpu

F32 = jnp.float32
BF16 = jnp.bfloat16

EPS = 1e-6
D_MODEL = 1024
D_FF = 2816
FFN_TAPS = 3
HALO = 8
VMEM_LIMIT_BYTES = 56 * 1024 * 1024


def _rmsnorm(x, g):
    ms = jnp.mean(x * x, axis=-1, keepdims=True)
    return x * lax.rsqrt(ms + EPS) * g


def _sigmoid(x):
    return 1.0 / (1.0 + jnp.exp(-x))


def _resident(shape):
    return pl.BlockSpec(shape, lambda *_: (0,) * len(shape), pipeline_mode=pl.Buffered(1))


FFN_T = 512
FFN_CH = 256
FFN_NCH = D_FF // FFN_CH


def _ffn_kernel(h_ref, g_ref, wup_ref, cw_ref, cb_ref, wdn_ref, fg_ref, o_ref,
                xn_ref, work0_ref, work1_ref, act0_ref, act1_ref, halo_ref, acc_ref, *, final_norm):
    T, CH = FFN_T, FFN_CH

    @pl.when(pl.program_id(1) == 0)
    def _():
        halo_ref[...] = jnp.zeros(halo_ref.shape, F32)

    h = h_ref[0]
    xn_ref[...] = _rmsnorm(h, g_ref[...]).astype(BF16)
    acc_ref[...] = h

    works = (work0_ref, work1_ref)
    acts = (act0_ref, act1_ref)

    def chunk_cols(j):
        return slice(j * CH, (j + 1) * CH), slice(D_FF + j * CH, D_FF + (j + 1) * CH)

    def up_proj(j):
        work_ref = works[j % 2]
        work_ref[0:HALO, :] = halo_ref[j]
        for half, cols in enumerate(chunk_cols(j)):
            work_ref[HALO:HALO + T, half * CH:(half + 1) * CH] = jnp.dot(
                xn_ref[...], wup_ref[:, cols], preferred_element_type=F32)
        halo_ref[j] = work_ref[T:T + HALO, :]

    def down_proj(j):
        acc_ref[...] += jnp.dot(acts[j % 2][...], wdn_ref[j * CH:(j + 1) * CH, :], preferred_element_type=F32)

    up_proj(0)
    for j in range(FFN_NCH):
        if j + 1 < FFN_NCH:
            up_proj(j + 1)
        if j >= 1:
            down_proj(j - 1)
        work_ref = works[j % 2]
        gate_cols, val_cols = chunk_cols(j)
        cw = jnp.concatenate([cw_ref[:, gate_cols], cw_ref[:, val_cols]], axis=1)
        y = jnp.concatenate([cb_ref[:, gate_cols], cb_ref[:, val_cols]], axis=1)
        for k in range(FFN_TAPS):
            off = HALO - (FFN_TAPS - 1) + k
            y = y + work_ref[off:off + T, :] * cw[k:k + 1, :]
        gate = y[:, :CH]
        val = y[:, CH:]
        acts[j % 2][...] = (gate * _sigmoid(gate) * val).astype(BF16)
    down_proj(FFN_NCH - 1)

    out = acc_ref[...]
    if final_norm:
        out = _rmsnorm(out, fg_ref[...])
    o_ref[0] = out


def _ffn(h, g, w_up, conv_w, conv_b, w_down, final_g, *, final_norm):
    B, S, D = h.shape
    T, CH, NCH = FFN_T, FFN_CH, FFN_NCH
    tile = pl.BlockSpec((1, T, D), lambda b, t: (b, t, 0))
    return pl.pallas_call(
        functools.partial(_ffn_kernel, final_norm=final_norm),
        out_shape=jax.ShapeDtypeStruct((B, S, D), F32),
        grid=(B, S // T),
        in_specs=[tile, _resident((1, D)), _resident((D, 2 * D_FF)), _resident((FFN_TAPS, 2 * D_FF)),
                  _resident((1, 2 * D_FF)), _resident((D_FF, D)), _resident((1, D))],
        out_specs=tile,
        scratch_shapes=[pltpu.VMEM((T, D), BF16),
                        pltpu.VMEM((T + HALO, 2 * CH), F32),
                        pltpu.VMEM((T + HALO, 2 * CH), F32),
                        pltpu.VMEM((T, CH), BF16),
                        pltpu.VMEM((T, CH), BF16),
                        pltpu.VMEM((NCH, HALO, 2 * CH), F32),
                        pltpu.VMEM((T, D), F32)],
        compiler_params=pltpu.CompilerParams(dimension_semantics=("arbitrary", "arbitrary"),
                                             vmem_limit_bytes=VMEM_LIMIT_BYTES),
        name="ffn_final" if final_norm else "ffn",
    )(h, g.reshape(1, D), w_up.astype(BF16), conv_w, conv_b.reshape(1, 2 * D_FF), w_down.astype(BF16),
      final_g.reshape(1, D))


EV_T = 512
MIX_W = 512
LRU_TAPS = 4
SC_TAPS = 3
LRU_DECAY_C = 8.0
GELU_C = 0.7978845608028654


def _gelu(x):
    return x * (0.5 * (1.0 + jnp.tanh(GELU_C * (x + 0.044715 * (x * x * x)))))


def _causal_taps(buf_ref, w_ref, b_ref, taps, T):
    w = w_ref[...]
    y = b_ref[...]
    for k in range(taps):
        off = HALO - (taps - 1) + k
        y = y + buf_ref[off:off + T, :] * w[k:k + 1, :]
    return y


def _even_kernel(h_ref, g_ref, win_ref, lcw_ref, lcb_ref, wg_ref, bg_ref, lam_ref, scw_ref, scb_ref, wout_ref,
                 o_ref, xn_ref, xa_ref, cv_ref, a_ref, hs_ref, y_ref, state_ref):
    T, W = EV_T, MIX_W

    @pl.when(pl.program_id(1) == 0)
    def _():
        xa_ref[0:HALO, :] = jnp.zeros((HALO, W), F32)
        cv_ref[0:HALO, :] = jnp.zeros((HALO, W), F32)
        state_ref[...] = jnp.zeros(state_ref.shape, F32)

    h = h_ref[0]
    xn_ref[...] = _rmsnorm(h, g_ref[...]).astype(BF16)

    def proj(c):
        return jnp.dot(xn_ref[...], win_ref[:, c * W:(c + 1) * W], preferred_element_type=F32)

    xa_ref[HALO:HALO + T, :] = proj(0)
    xc = _causal_taps(xa_ref, lcw_ref, lcb_ref, LRU_TAPS, T)
    xa_ref[0:HALO, :] = xa_ref[T:T + HALO, :]
    pre = jnp.dot(xc.astype(BF16), wg_ref[...], preferred_element_type=F32) + bg_ref[...]
    r = _sigmoid(pre[:, :W])
    i = _sigmoid(pre[:, W:])
    lam = lam_ref[...]
    softplus_neg_lam = jnp.maximum(-lam, 0.0) + jnp.log1p(jnp.exp(-jnp.abs(lam)))
    log_a = (-LRU_DECAY_C * r) * softplus_neg_lam
    a = jnp.exp(log_a)
    th = jnp.tanh(log_a)
    u = jnp.sqrt((-2.0 * th) / (1.0 - th)) * (i * xc)

    G = T // 8
    a3 = a.reshape(G, 8, W)
    u3 = u.reshape(G, 8, W)
    row = lax.broadcasted_iota(jnp.int32, (G, 8, W), 1)
    for d in (1, 2, 4):
        keep = row >= d
        u3 = u3 + a3 * jnp.where(keep, pltpu.roll(u3, d, axis=1), 0.0)
        a3 = a3 * jnp.where(keep, pltpu.roll(a3, d, axis=1), 1.0)
    a_ref[...] = a3.reshape(T, W)
    hs_ref[...] = u3.reshape(T, W)
    carry = state_ref[0:1, :]
    for gi in range(G):
        rows = slice(gi * 8, gi * 8 + 8)
        hg = hs_ref[rows, :] + a_ref[rows, :] * carry
        hs_ref[rows, :] = hg
        carry = hg[7:8, :]
    state_ref[0:1, :] = carry
    y_ref[:, 0:W] = (hs_ref[...] * _gelu(proj(1))).astype(BF16)

    cv_ref[HALO:HALO + T, :] = proj(2) * proj(4)
    cc = _causal_taps(cv_ref, scw_ref, scb_ref, SC_TAPS, T)
    cv_ref[0:HALO, :] = cv_ref[T:T + HALO, :]
    y_ref[:, W:2 * W] = (proj(3) * cc).astype(BF16)

    o_ref[0] = h + jnp.dot(y_ref[...], wout_ref[...], preferred_element_type=F32)


def _block_diag(w):
    H, d, _ = w.shape
    return jnp.einsum('hij,hg->higj', w, jnp.eye(H, dtype=w.dtype)).reshape(H * d, H * d)


def _even_mixer(h, g, w_in, lcw, lcb, wa, ba, wx, bx, lam, scw, scb, w_out):
    B, S, D = h.shape
    T, W = EV_T, MIX_W
    wg = jnp.concatenate([_block_diag(wa), _block_diag(wx)], axis=1).astype(BF16)
    bg = jnp.concatenate([ba, bx]).reshape(1, 2 * W)
    tile = pl.BlockSpec((1, T, D), lambda b, t: (b, t, 0))
    return pl.pallas_call(
        _even_kernel,
        out_shape=jax.ShapeDtypeStruct((B, S, D), F32),
        grid=(B, S // T),
        in_specs=[tile, _resident((1, D)), _resident((D, 5 * W)), _resident((LRU_TAPS, W)), _resident((1, W)),
                  _resident((W, 2 * W)), _resident((1, 2 * W)), _resident((1, W)), _resident((SC_TAPS, W)),
                  _resident((1, W)), _resident((2 * W, D))],
        out_specs=tile,
        scratch_shapes=[pltpu.VMEM((T, D), BF16),
                        pltpu.VMEM((T + HALO, W), F32),
                        pltpu.VMEM((T + HALO, W), F32),
                        pltpu.VMEM((T, W), F32),
                        pltpu.VMEM((T, W), F32),
                        pltpu.VMEM((T, 2 * W), BF16),
                        pltpu.VMEM((8, W), F32)],
        compiler_params=pltpu.CompilerParams(dimension_semantics=("arbitrary", "arbitrary"),
                                             vmem_limit_bytes=VMEM_LIMIT_BYTES),
        name="even_mixer",
    )(h, g.reshape(1, D), w_in.astype(BF16), lcw, lcb.reshape(1, W), wg, bg, lam.reshape(1, W), scw,
      scb.reshape(1, W), w_out.astype(BF16))


OD_T = 512
SGU_CHUNK = 128
N_HEADS = 8
HEAD_DIM = 64
F_PAD = 128
HEADS_PER_HALF = 4
QK_SCALE = HEAD_DIM ** -0.5
LOG2E = 1.4426950408889634
PAIR_W = 2 * HEAD_DIM
N_PAIRS = N_HEADS // 2


def _split3_bf16(x):
    hi = x.astype(BF16)
    r1 = x - hi.astype(F32)
    mid = r1.astype(BF16)
    lo = (r1 - mid.astype(F32)).astype(BF16)
    return hi, mid, lo


def _odd_in_kernel(h_ref, g_ref, win_ref, gn_ref, seg_ref, sw_ref, sb_ref, bf_ref,
                   yc_ref, q_ref, k_ref, v_ref, st_ref, xn_ref, gv_ref, u_ref, csum_ref):
    T, W = OD_T, MIX_W

    @pl.when(pl.program_id(1) == 0)
    def _():
        csum_ref[...] = jnp.zeros(csum_ref.shape, F32)

    xn_ref[...] = _rmsnorm(h_ref[0], g_ref[...]).astype(BF16)

    def proj(lo, width):
        return jnp.dot(xn_ref[...], win_ref[:, lo:lo + width], preferred_element_type=F32)

    fl = proj(5 * W, F_PAD) + bf_ref[...]
    logf = jnp.minimum(fl, 0.0) - jnp.log1p(jnp.exp(-jnp.abs(fl)))
    tri = (lax.broadcasted_iota(jnp.int32, (T, T), 1) <= lax.broadcasted_iota(jnp.int32, (T, T), 0)).astype(BF16)
    c_loc = sum(jnp.dot(tri, part, preferred_element_type=F32) for part in _split3_bf16(logf))
    c_tile = c_loc + csum_ref[0:1, :]
    csum_ref[0:1, :] = c_tile[T - 1:T, :]

    lane = lax.broadcasted_iota(jnp.int32, (T, PAIR_W), 1)
    low = lane < HEAD_DIM
    pq = proj(2 * W, W) * (QK_SCALE * LOG2E)
    pk = proj(3 * W, W)
    pv = proj(4 * W, W)
    for hh in range(N_HEADS):
        blk = slice((hh // 2) * PAIR_W, (hh // 2 + 1) * PAIR_W)
        own = low if hh % 2 == 0 else jnp.logical_not(low)
        a0 = HEAD_DIM if hh % 2 == 0 else 0
        x = jnp.broadcast_to(c_tile[:, hh:hh + 1] * (-LOG2E), (T, PAIR_W))
        hi = x.astype(BF16).astype(F32)
        r1 = x - hi
        mid = r1.astype(BF16).astype(F32)
        lo = r1 - mid
        c_parts = jnp.where(lane == a0, hi, jnp.where(lane == a0 + 1, mid, jnp.where(lane == a0 + 2, lo, 0.0)))
        ones3 = jnp.where((lane >= a0) & (lane < a0 + 3), 1.0, 0.0)
        one1 = jnp.where(lane == a0, 1.0, 0.0)
        q_ref[0, hh] = jnp.where(own, pq[:, blk], ones3).astype(BF16)
        k_ref[0, hh] = jnp.where(own, pk[:, blk], c_parts).astype(BF16)
        v_ref[0, hh] = jnp.where(own, pv[:, blk], one1).astype(BF16)

    def max_block_norm(x):
        sq = [jnp.max(jnp.sum(x[:, p * PAIR_W:(p + 1) * PAIR_W] ** 2, axis=-1, keepdims=True), axis=0, keepdims=True)
              for p in range(N_PAIRS)]
        return jnp.broadcast_to(jnp.sqrt(functools.reduce(jnp.maximum, sq)), (1, PAIR_W))

    st_ref[0, 0] = jnp.concatenate([c_tile[0:1, :], c_tile[T - 1:T, :], max_block_norm(pq), max_block_norm(pk),
                                    jnp.zeros((4, PAIR_W), F32)], axis=0)

    u_ref[...] = _gelu(proj(0, W))
    g = _gelu(proj(W, W))
    gsq = g * g
    gsq_hi = gsq.astype(BF16)
    gsq_lo = (gsq - gsq_hi.astype(F32)).astype(BF16)
    ms = (jnp.dot(gsq_hi, seg_ref[...], preferred_element_type=F32)
          + jnp.dot(gsq_lo, seg_ref[...], preferred_element_type=F32)) * (1.0 / HEAD_DIM)
    gv_ref[...] = (g * lax.rsqrt(ms + EPS) * gn_ref[...]).astype(BF16)

    HW = HEADS_PER_HALF * HEAD_DIM
    lane_head = lax.broadcasted_iota(jnp.int32, (SGU_CHUNK, HW), 1) // HEAD_DIM
    wrow = lax.broadcasted_iota(jnp.int32, (SGU_CHUNK, HEADS_PER_HALF * SGU_CHUNK), 0)
    wcol = lax.broadcasted_iota(jnp.int32, (SGU_CHUNK, HEADS_PER_HALF * SGU_CHUNK), 1) % SGU_CHUNK
    for half in range(2):
        cols = slice(half * HW, (half + 1) * HW)
        w_causal = jnp.where(wcol <= wrow, sw_ref[half], 0.0).astype(BF16)
        bias = sb_ref[:, cols]
        for c in range(T // SGU_CHUNK):
            rows = slice(c * SGU_CHUNK, (c + 1) * SGU_CHUNK)
            gvc = gv_ref[rows, cols]
            stacked = jnp.concatenate(
                [jnp.where(lane_head == j, gvc, jnp.zeros_like(gvc)) for j in range(HEADS_PER_HALF)], axis=0)
            mixed = jnp.dot(w_causal, stacked, preferred_element_type=F32) + bias
            yc_ref[0, rows, cols] = (u_ref[rows, cols] * mixed).astype(BF16)


def _odd_in(h, g, w_in, sgu_norm, sgu_w, sgu_b, fox_bf):
    B, S, D = h.shape
    T, W = OD_T, MIX_W
    n_in = 5 * W + F_PAD
    win = jnp.pad(w_in, ((0, 0), (0, n_in - w_in.shape[1]))).astype(BF16)
    seg = _block_diag(jnp.ones((N_HEADS, HEAD_DIM, HEAD_DIM), F32)).astype(BF16)
    sw = sgu_w.reshape(2, HEADS_PER_HALF, SGU_CHUNK, SGU_CHUNK).transpose(0, 2, 1, 3).reshape(
        2, SGU_CHUNK, HEADS_PER_HALF * SGU_CHUNK)
    sb = jnp.repeat(sgu_b.T, HEAD_DIM, axis=1)
    bf = jnp.pad(fox_bf, (0, F_PAD - N_HEADS)).reshape(1, F_PAD)
    tile = pl.BlockSpec((1, T, D), lambda b, t: (b, t, 0))
    half_tile = pl.BlockSpec((1, T, W), lambda b, t: (b, t, 0))
    head_tile = pl.BlockSpec((1, N_HEADS, T, PAIR_W), lambda b, t: (b, 0, t, 0))
    act = jax.ShapeDtypeStruct((B, S, W), BF16)
    per_head = jax.ShapeDtypeStruct((B, N_HEADS, S, PAIR_W), BF16)
    return pl.pallas_call(
        _odd_in_kernel,
        out_shape=(act, per_head, per_head, per_head, jax.ShapeDtypeStruct((B, S // T, 8, PAIR_W), F32)),
        grid=(B, S // T),
        in_specs=[tile, _resident((1, D)), _resident((D, n_in)), _resident((1, W)), _resident((W, W)),
                  _resident((2, SGU_CHUNK, HEADS_PER_HALF * SGU_CHUNK)), _resident((SGU_CHUNK, W)),
                  _resident((1, F_PAD))],
        out_specs=(half_tile, head_tile, head_tile, head_tile,
                   pl.BlockSpec((1, 1, 8, PAIR_W), lambda b, t: (b, t, 0, 0))),
        scratch_shapes=[pltpu.VMEM((T, D), BF16),
                        pltpu.VMEM((T, W), BF16),
                        pltpu.VMEM((T, W), F32),
                        pltpu.VMEM((N_HEADS, 128), F32)],
        compiler_params=pltpu.CompilerParams(dimension_semantics=("arbitrary", "arbitrary"),
                                             vmem_limit_bytes=VMEM_LIMIT_BYTES),
        name="odd_in",
    )(h, g.reshape(1, D), win, sgu_norm.reshape(1, W), seg, sw, sb, bf)


AT_T = 512
M_INIT = -1e30
QK_AHEAD = 2
SKIP_LOG2 = -160.0
NORM_SLACK = 1.05


def _first_needed_block(stats):
    c_first = stats[:, :, 0, :N_HEADS]
    c_last = stats[:, :, 1, :N_HEADS]
    q_norm = stats[:, :, 2, 0]
    k_norm = jnp.max(stats[:, :, 3, 0], axis=1, keepdims=True)
    spread = NORM_SLACK * 2.0 * q_norm * k_norm
    decay = (c_last[:, None, :, :] - c_first[:, :, None, :]) * LOG2E
    n = stats.shape[1]
    earlier = jnp.arange(n)[None, :] < jnp.arange(n)[:, None]
    skip = jnp.all(spread[:, :, None, None] - decay < SKIP_LOG2, axis=-1) & earlier[None]
    return jnp.sum(jnp.cumprod(skip.astype(jnp.int32), axis=-1), axis=-1).astype(jnp.int32)


def _attn_kernel(first_ref, q_ref, k_ref, v_ref, yc_ref, h_ref, wout_ref, o_ref,
                 s_ref, mx_ref, m_ref, acc_ref, yd_ref):
    T, W = AT_T, MIX_W
    qi = pl.program_id(1)
    m_ref[...] = jnp.full(m_ref.shape, M_INIT, F32)
    acc_ref[...] = jnp.zeros(acc_ref.shape, F32)

    def block(j, diagonal):
        k0 = pl.multiple_of(j * T, T)
        if diagonal:
            visible = (lax.broadcasted_iota(jnp.int32, (T, T), 1) <= lax.broadcasted_iota(jnp.int32, (T, T), 0))
        def logits(hh):
            s = lax.dot_general(q_ref[0, hh], k_ref[0, hh, pl.ds(k0, T), :], (((1,), (1,)), ((), ())),
                                preferred_element_type=F32)
            if diagonal:
                s = jnp.where(visible, s, -jnp.inf)
            s_ref[hh] = s
            mx_ref[hh] = jnp.broadcast_to(jnp.max(s, axis=-1, keepdims=True), (T, PAIR_W))

        for hh in range(QK_AHEAD):
            logits(hh)
        for hh in range(N_HEADS):
            if hh + QK_AHEAD < N_HEADS:
                logits(hh + QK_AHEAD)
            m_old = m_ref[hh]
            m_new = jnp.maximum(m_old, mx_ref[hh])
            alpha = jnp.exp2(m_old - m_new)
            p = jnp.exp2(s_ref[hh] - jnp.concatenate([m_new] * (T // PAIR_W), axis=1))
            m_ref[hh] = m_new
            acc_ref[hh] = alpha * acc_ref[hh] + jnp.dot(p.astype(BF16), v_ref[0, hh, pl.ds(k0, T), :],
                                                        preferred_element_type=F32)

    def off_diagonal(j, carry):
        block(j, False)
        return carry

    lax.fori_loop(first_ref[pl.program_id(0), qi], qi, off_diagonal, 0)
    block(qi, True)

    low = lax.broadcasted_iota(jnp.int32, (T, PAIR_W), 1) < HEAD_DIM
    for pair in range(N_PAIRS):
        acc_a = acc_ref[2 * pair]
        acc_b = acc_ref[2 * pair + 1]
        y = jnp.where(low, acc_a / acc_a[:, HEAD_DIM:HEAD_DIM + 1], acc_b / acc_b[:, 0:1])
        yd_ref[:, pair * PAIR_W:(pair + 1) * PAIR_W] = y.astype(BF16)
    o_ref[0] = (h_ref[0]
                + jnp.dot(yc_ref[0], wout_ref[0:W, :], preferred_element_type=F32)
                + jnp.dot(yd_ref[...], wout_ref[W:2 * W, :], preferred_element_type=F32))


def _attention(q, k, v, stats, yc, h, w_out):
    B, S, D = h.shape
    T, W = AT_T, MIX_W
    assert OD_T == AT_T, "the tile summaries are per odd_in tile"
    q_tile = pl.BlockSpec((1, N_HEADS, T, PAIR_W), lambda b, t: (b, 0, t, 0))
    seq = pl.BlockSpec((1, N_HEADS, S, PAIR_W), lambda b, t: (b, 0, 0, 0), pipeline_mode=pl.Buffered(1))
    tile = pl.BlockSpec((1, T, D), lambda b, t: (b, t, 0))
    return pl.pallas_call(
        _attn_kernel,
        out_shape=jax.ShapeDtypeStruct((B, S, D), F32),
        grid=(B, S // T),
        in_specs=[pl.BlockSpec(memory_space=pltpu.SMEM), q_tile, seq, seq,
                  pl.BlockSpec((1, T, W), lambda b, t: (b, t, 0)), tile, _resident((2 * W, D))],
        out_specs=tile,
        scratch_shapes=[pltpu.VMEM((N_HEADS, T, T), F32),
                        pltpu.VMEM((N_HEADS, T, PAIR_W), F32),
                        pltpu.VMEM((N_HEADS, T, PAIR_W), F32),
                        pltpu.VMEM((N_HEADS, T, PAIR_W), F32),
                        pltpu.VMEM((T, W), BF16)],
        compiler_params=pltpu.CompilerParams(dimension_semantics=("arbitrary", "arbitrary"),
                                             vmem_limit_bytes=VMEM_LIMIT_BYTES),
        name="fox_attention",
    )(_first_needed_block(stats), q, k, v, yc, h, w_out.astype(BF16))


def kernel(x, mix0_norm, mix0_w_in, lru_conv_w, lru_conv_b, lru_wa, lru_ba, lru_wx, lru_bx, lru_lambda,
           sconv_w, sconv_b, mix0_w_out, mix1_norm, mix1_w_in, sgu_norm, sgu_w, sgu_b, fox_bf, mix1_w_out,
           ffn_norm, ffn_up, ffn_conv_w, ffn_conv_b, ffn_down, final_norm):
    h = x
    h = _even_mixer(h, mix0_norm[0], mix0_w_in[0], lru_conv_w[0], lru_conv_b[0], lru_wa[0], lru_ba[0],
                    lru_wx[0], lru_bx[0], lru_lambda[0], sconv_w[0], sconv_b[0], mix0_w_out[0])
    h = _ffn(h, ffn_norm[0], ffn_up[0], ffn_conv_w[0], ffn_conv_b[0], ffn_down[0], final_norm, final_norm=False)
    yc, q, k, v, stats = _odd_in(h, mix1_norm[0], mix1_w_in[0], sgu_norm[0], sgu_w[0], sgu_b[0], fox_bf[0])
    h = _attention(q, k, v, stats, yc, h, mix1_w_out[0])
    h = _ffn(h, ffn_norm[1], ffn_up[1], ffn_conv_w[1], ffn_conv_b[1], ffn_down[1], final_norm, final_norm=True)
    return h
```

```python
import functools

import jax
import jax.numpy as jnp
from jax import lax
from jax.experimental import pallas as pl
from jax.experimental.pallas import tpu as pltpu

F32 = jnp.float32
BF16 = jnp.bfloat16

EPS = 1e-6
D_MODEL = 1024
D_FF = 2816
FFN_TAPS = 3
HALO = 8
VMEM_LIMIT_BYTES = 56 * 1024 * 1024


def _rmsnorm(x, g):
    ms = jnp.mean(x * x, axis=-1, keepdims=True)
    return x * lax.rsqrt(ms + EPS) * g


def _sigmoid(x):
    return 1.0 / (1.0 + jnp.exp(-x))


def _resident(shape):
    return pl.BlockSpec(shape, lambda *_: (0,) * len(shape), pipeline_mode=pl.Buffered(1))


FFN_T = 512
FFN_CH = 256
FFN_NCH = D_FF // FFN_CH
SUBLANES = 8
FFN_PAD = (FFN_TAPS - 1) * SUBLANES


def _ffn_kernel(h_hbm, g_ref, wup_ref, cw_ref, cb_ref, wdn_ref, fg_ref, o_hbm,
                hbuf_ref, obuf_ref, xn_ref, work0_ref, work1_ref, act0_ref, act1_ref, halo_ref, acc_ref,
                sem_in, sem_out, *, final_norm, n_tiles, n_steps):
    T, CH, PAD = FFN_T, FFN_CH, FFN_PAD
    G = T // SUBLANES
    b = pl.program_id(0)
    t = pl.program_id(1)
    n = b * n_tiles + t
    slot = lax.rem(n, 2)

    def in_copies(bb, tt, sl):
        return [pltpu.make_async_copy(h_hbm.at[bb, pl.ds(tt * T + r * G, G), :], hbuf_ref.at[sl, :, r, :],
                                      sem_in.at[sl]) for r in range(SUBLANES)]

    def out_copies(bb, tt, sl):
        return [pltpu.make_async_copy(obuf_ref.at[sl, :, r, :], o_hbm.at[bb, pl.ds(tt * T + r * G, G), :],
                                      sem_out.at[sl]) for r in range(SUBLANES)]

    @pl.when(n == 0)
    def _():
        for c in in_copies(b, t, slot):
            c.start()

    @pl.when(n + 1 < n_steps)
    def _():
        n1 = n + 1
        for c in in_copies(n1 // n_tiles, lax.rem(n1, n_tiles), 1 - slot):
            c.start()

    @pl.when(t == 0)
    def _():
        halo_ref[...] = jnp.zeros(halo_ref.shape, F32)

    for c in in_copies(b, t, slot):
        c.wait()
    h = hbuf_ref[slot].reshape(T, D_MODEL)
    xn_ref[...] = _rmsnorm(h, g_ref[...]).astype(BF16)
    acc_ref[...] = h

    works = (work0_ref, work1_ref)
    acts = (act0_ref, act1_ref)
    first_row = lax.broadcasted_iota(jnp.int32, (SUBLANES, CH), 0) == 0

    def chunk_cols(j):
        return slice(j * CH, (j + 1) * CH), slice(D_FF + j * CH, D_FF + (j + 1) * CH)

    def up_proj_half(j, half):
        work_ref = works[j % 2]
        hcols = slice(half * CH, (half + 1) * CH)
        work_ref[PAD:PAD + T, hcols] = jnp.dot(xn_ref[...], wup_ref[:, chunk_cols(j)[half]],
                                               preferred_element_type=F32)
        for k in range(FFN_TAPS - 1):
            rows = slice(k * SUBLANES, (k + 1) * SUBLANES)
            tail = pltpu.roll(work_ref[T + k * SUBLANES:T + (k + 1) * SUBLANES, hcols], 1, axis=0)
            work_ref[rows, hcols] = jnp.where(first_row, halo_ref[j, rows, hcols], tail)
            halo_ref[j, rows, hcols] = tail

    def down_proj_half(j, half):
        cols = slice(half * (D_MODEL // 2), (half + 1) * (D_MODEL // 2))
        acc_ref[:, cols] += jnp.dot(acts[j % 2][...], wdn_ref[j * CH:(j + 1) * CH, cols],
                                    preferred_element_type=F32)

    up_proj_half(0, 0)
    up_proj_half(0, 1)
    for j in range(FFN_NCH):
        for half in range(2):
            if j + 1 < FFN_NCH:
                up_proj_half(j + 1, half)
            if j >= 1:
                down_proj_half(j - 1, half)
        work_ref = works[j % 2]
        gate_cols, val_cols = chunk_cols(j)
        cw = jnp.concatenate([cw_ref[:, gate_cols], cw_ref[:, val_cols]], axis=1)
        y = jnp.concatenate([cb_ref[:, gate_cols], cb_ref[:, val_cols]], axis=1)
        for k in range(FFN_TAPS):
            y = y + work_ref[k * SUBLANES:k * SUBLANES + T, :] * cw[k:k + 1, :]
        gate = y[:, :CH]
        val = y[:, CH:]
        acts[j % 2][...] = (gate * _sigmoid(gate) * val).astype(BF16)
    down_proj_half(FFN_NCH - 1, 0)
    down_proj_half(FFN_NCH - 1, 1)

    out = acc_ref[...]
    if final_norm:
        out = _rmsnorm(out, fg_ref[...])

    @pl.when(n >= 2)
    def _():
        for c in out_copies(b, t, slot):
            c.wait()

    obuf_ref[slot] = out.reshape(G, SUBLANES, D_MODEL)
    for c in out_copies(b, t, slot):
        c.start()

    @pl.when(n == n_steps - 1)
    def _():
        for c in out_copies(b, t, slot):
            c.wait()
        if n_steps >= 2:
            for c in out_copies(b, t, 1 - slot):
                c.wait()


def _ffn(h, g, w_up, conv_w, conv_b, w_down, final_g, *, final_norm):
    B, S, D = h.shape
    T, CH, NCH = FFN_T, FFN_CH, FFN_NCH
    G = T // SUBLANES
    n_tiles = S // T
    return pl.pallas_call(
        functools.partial(_ffn_kernel, final_norm=final_norm, n_tiles=n_tiles, n_steps=B * n_tiles),
        out_shape=jax.ShapeDtypeStruct((B, S, D), F32),
        grid=(B, n_tiles),
        in_specs=[pl.BlockSpec(memory_space=pl.ANY), _resident((1, D)), _resident((D, 2 * D_FF)),
                  _resident((FFN_TAPS, 2 * D_FF)), _resident((1, 2 * D_FF)), _resident((D_FF, D)),
                  _resident((1, D))],
        out_specs=pl.BlockSpec(memory_space=pl.ANY),
        scratch_shapes=[pltpu.VMEM((2, G, SUBLANES, D), F32),
                        pltpu.VMEM((2, G, SUBLANES, D), F32),
                        pltpu.VMEM((T, D), BF16),
                        pltpu.VMEM((T + FFN_PAD, 2 * CH), F32),
                        pltpu.VMEM((T + FFN_PAD, 2 * CH), F32),
                        pltpu.VMEM((T, CH), BF16),
                        pltpu.VMEM((T, CH), BF16),
                        pltpu.VMEM((NCH, FFN_PAD, 2 * CH), F32),
                        pltpu.VMEM((T, D), F32),
                        pltpu.SemaphoreType.DMA((2,)),
                        pltpu.SemaphoreType.DMA((2,))],
        compiler_params=pltpu.CompilerParams(dimension_semantics=("arbitrary", "arbitrary"),
                                             vmem_limit_bytes=VMEM_LIMIT_BYTES),
        name="ffn_final" if final_norm else "ffn",
    )(h, g.reshape(1, D), w_up.astype(BF16), conv_w, conv_b.reshape(1, 2 * D_FF), w_down.astype(BF16),
      final_g.reshape(1, D))


EV_T = 512
MIX_W = 512
LRU_TAPS = 4
SC_TAPS = 3
LRU_DECAY_C = 8.0
GELU_C = 0.7978845608028654


def _gelu(x):
    return x * (0.5 * (1.0 + jnp.tanh(GELU_C * (x + 0.044715 * (x * x * x)))))


def _causal_taps(buf_ref, w_ref, b_ref, taps, T):
    w = w_ref[...]
    y = b_ref[...]
    for k in range(taps):
        off = HALO - (taps - 1) + k
        y = y + buf_ref[off:off + T, :] * w[k:k + 1, :]
    return y


def _even_kernel(h_ref, g_ref, win_ref, lcw_ref, lcb_ref, wg_ref, bg_ref, lam_ref, scw_ref, scb_ref, wout_ref,
                 o_ref, xn_ref, xa_ref, cv_ref, a_ref, hs_ref, y_ref, state_ref):
    T, W = EV_T, MIX_W

    @pl.when(pl.program_id(1) == 0)
    def _():
        xa_ref[0:HALO, :] = jnp.zeros((HALO, W), F32)
        cv_ref[0:HALO, :] = jnp.zeros((HALO, W), F32)
        state_ref[...] = jnp.zeros(state_ref.shape, F32)

    h = h_ref[0]
    xn_ref[...] = _rmsnorm(h, g_ref[...]).astype(BF16)

    def proj(c):
        return jnp.dot(xn_ref[...], win_ref[:, c * W:(c + 1) * W], preferred_element_type=F32)

    xa_ref[HALO:HALO + T, :] = proj(0)
    xc = _causal_taps(xa_ref, lcw_ref, lcb_ref, LRU_TAPS, T)
    xa_ref[0:HALO, :] = xa_ref[T:T + HALO, :]
    pre = jnp.dot(xc.astype(BF16), wg_ref[...], preferred_element_type=F32) + bg_ref[...]
    r = _sigmoid(pre[:, :W])
    i = _sigmoid(pre[:, W:])
    lam = lam_ref[...]
    softplus_neg_lam = jnp.maximum(-lam, 0.0) + jnp.log1p(jnp.exp(-jnp.abs(lam)))
    log_a = (-LRU_DECAY_C * r) * softplus_neg_lam
    a = jnp.exp(log_a)
    th = jnp.tanh(log_a)
    u = jnp.sqrt((-2.0 * th) / (1.0 - th)) * (i * xc)

    G = T // 8
    a3 = a.reshape(G, 8, W)
    u3 = u.reshape(G, 8, W)
    row = lax.broadcasted_iota(jnp.int32, (G, 8, W), 1)
    for d in (1, 2, 4):
        keep = row >= d
        u3 = u3 + a3 * jnp.where(keep, pltpu.roll(u3, d, axis=1), 0.0)
        a3 = a3 * jnp.where(keep, pltpu.roll(a3, d, axis=1), 1.0)
    a_ref[...] = a3.reshape(T, W)
    hs_ref[...] = u3.reshape(T, W)
    carry = state_ref[0:1, :]
    for gi in range(G):
        rows = slice(gi * 8, gi * 8 + 8)
        hg = hs_ref[rows, :] + a_ref[rows, :] * carry
        hs_ref[rows, :] = hg
        carry = hg[7:8, :]
    state_ref[0:1, :] = carry
    y_ref[:, 0:W] = (hs_ref[...] * _gelu(proj(1))).astype(BF16)

    cv_ref[HALO:HALO + T, :] = proj(2) * proj(4)
    cc = _causal_taps(cv_ref, scw_ref, scb_ref, SC_TAPS, T)
    cv_ref[0:HALO, :] = cv_ref[T:T + HALO, :]
    y_ref[:, W:2 * W] = (proj(3) * cc).astype(BF16)

    o_ref[0] = h + jnp.dot(y_ref[...], wout_ref[...], preferred_element_type=F32)


def _block_diag(w):
    H, d, _ = w.shape
    return jnp.einsum('hij,hg->higj', w, jnp.eye(H, dtype=w.dtype)).reshape(H * d, H * d)


def _even_mixer(h, g, w_in, lcw, lcb, wa, ba, wx, bx, lam, scw, scb, w_out):
    B, S, D = h.shape
    T, W = EV_T, MIX_W
    wg = jnp.concatenate([_block_diag(wa), _block_diag(wx)], axis=1).astype(BF16)
    bg = jnp.concatenate([ba, bx]).reshape(1, 2 * W)
    tile = pl.BlockSpec((1, T, D), lambda b, t: (b, t, 0))
    return pl.pallas_call(
        _even_kernel,
        out_shape=jax.ShapeDtypeStruct((B, S, D), F32),
        grid=(B, S // T),
        in_specs=[tile, _resident((1, D)), _resident((D, 5 * W)), _resident((LRU_TAPS, W)), _resident((1, W)),
                  _resident((W, 2 * W)), _resident((1, 2 * W)), _resident((1, W)), _resident((SC_TAPS, W)),
                  _resident((1, W)), _resident((2 * W, D))],
        out_specs=tile,
        scratch_shapes=[pltpu.VMEM((T, D), BF16),
                        pltpu.VMEM((T + HALO, W), F32),
                        pltpu.VMEM((T + HALO, W), F32),
                        pltpu.VMEM((T, W), F32),
                        pltpu.VMEM((T, W), F32),
                        pltpu.VMEM((T, 2 * W), BF16),
                        pltpu.VMEM((8, W), F32)],
        compiler_params=pltpu.CompilerParams(dimension_semantics=("arbitrary", "arbitrary"),
                                             vmem_limit_bytes=VMEM_LIMIT_BYTES),
        name="even_mixer",
    )(h, g.reshape(1, D), w_in.astype(BF16), lcw, lcb.reshape(1, W), wg, bg, lam.reshape(1, W), scw,
      scb.reshape(1, W), w_out.astype(BF16))


OD_T = 512
SGU_CHUNK = 128
N_HEADS = 8
HEAD_DIM = 64
F_PAD = 128
HEADS_PER_HALF = 4
QK_SCALE = HEAD_DIM ** -0.5
LOG2E = 1.4426950408889634
PAIR_W = 2 * HEAD_DIM
N_PAIRS = N_HEADS // 2


def _split3_bf16(x):
    hi = x.astype(BF16)
    r1 = x - hi.astype(F32)
    mid = r1.astype(BF16)
    lo = (r1 - mid.astype(F32)).astype(BF16)
    return hi, mid, lo


def _odd_in_kernel(h_ref, g_ref, win_ref, gn_ref, seg_ref, sw_ref, sb_ref, bf_ref,
                   yc_ref, q_ref, k_ref, v_ref, st_ref, xn_ref, gv_ref, u_ref, csum_ref):
    T, W = OD_T, MIX_W

    @pl.when(pl.program_id(1) == 0)
    def _():
        csum_ref[...] = jnp.zeros(csum_ref.shape, F32)

    xn_ref[...] = _rmsnorm(h_ref[0], g_ref[...]).astype(BF16)

    def proj(lo, width):
        return jnp.dot(xn_ref[...], win_ref[:, lo:lo + width], preferred_element_type=F32)

    fl = proj(5 * W, F_PAD) + bf_ref[...]
    logf = jnp.minimum(fl, 0.0) - jnp.log1p(jnp.exp(-jnp.abs(fl)))
    tri = (lax.broadcasted_iota(jnp.int32, (T, T), 1) <= lax.broadcasted_iota(jnp.int32, (T, T), 0)).astype(BF16)
    c_loc = sum(jnp.dot(tri, part, preferred_element_type=F32) for part in _split3_bf16(logf))
    c_tile = c_loc + csum_ref[0:1, :]
    csum_ref[0:1, :] = c_tile[T - 1:T, :]

    lane = lax.broadcasted_iota(jnp.int32, (T, PAIR_W), 1)
    low = lane < HEAD_DIM
    pq = proj(2 * W, W) * (QK_SCALE * LOG2E)
    pk = proj(3 * W, W)
    pv = proj(4 * W, W)
    for hh in range(N_HEADS):
        blk = slice((hh // 2) * PAIR_W, (hh // 2 + 1) * PAIR_W)
        own = low if hh % 2 == 0 else jnp.logical_not(low)
        a0 = HEAD_DIM if hh % 2 == 0 else 0
        x = jnp.broadcast_to(c_tile[:, hh:hh + 1] * (-LOG2E), (T, PAIR_W))
        hi = x.astype(BF16).astype(F32)
        r1 = x - hi
        mid = r1.astype(BF16).astype(F32)
        lo = r1 - mid
        c_parts = jnp.where(lane == a0, hi, jnp.where(lane == a0 + 1, mid, jnp.where(lane == a0 + 2, lo, 0.0)))
        ones3 = jnp.where((lane >= a0) & (lane < a0 + 3), 1.0, 0.0)
        one1 = jnp.where(lane == a0, 1.0, 0.0)
        q_ref[0, hh] = jnp.where(own, pq[:, blk], ones3).astype(BF16)
        k_ref[0, hh] = jnp.where(own, pk[:, blk], c_parts).astype(BF16)
        v_ref[0, hh] = jnp.where(own, pv[:, blk], one1).astype(BF16)

    def max_block_norm(x):
        sq = [jnp.max(jnp.sum(x[:, p * PAIR_W:(p + 1) * PAIR_W] ** 2, axis=-1, keepdims=True), axis=0, keepdims=True)
              for p in range(N_PAIRS)]
        return jnp.broadcast_to(jnp.sqrt(functools.reduce(jnp.maximum, sq)), (1, PAIR_W))

    st_ref[0, 0] = jnp.concatenate([c_tile[0:1, :], c_tile[T - 1:T, :], max_block_norm(pq), max_block_norm(pk),
                                    jnp.zeros((4, PAIR_W), F32)], axis=0)

    u_ref[...] = _gelu(proj(0, W))
    g = _gelu(proj(W, W))
    gsq = g * g
    gsq_hi = gsq.astype(BF16)
    gsq_lo = (gsq - gsq_hi.astype(F32)).astype(BF16)
    ms = (jnp.dot(gsq_hi, seg_ref[...], preferred_element_type=F32)
          + jnp.dot(gsq_lo, seg_ref[...], preferred_element_type=F32)) * (1.0 / HEAD_DIM)
    gv_ref[...] = (g * lax.rsqrt(ms + EPS) * gn_ref[...]).astype(BF16)

    HW = HEADS_PER_HALF * HEAD_DIM
    lane_head = lax.broadcasted_iota(jnp.int32, (SGU_CHUNK, HW), 1) // HEAD_DIM
    wrow = lax.broadcasted_iota(jnp.int32, (SGU_CHUNK, HEADS_PER_HALF * SGU_CHUNK), 0)
    wcol = lax.broadcasted_iota(jnp.int32, (SGU_CHUNK, HEADS_PER_HALF * SGU_CHUNK), 1) % SGU_CHUNK
    for half in range(2):
        cols = slice(half * HW, (half + 1) * HW)
        w_causal = jnp.where(wcol <= wrow, sw_ref[half], 0.0).astype(BF16)
        bias = sb_ref[:, cols]
        for c in range(T // SGU_CHUNK):
            rows = slice(c * SGU_CHUNK, (c + 1) * SGU_CHUNK)
            gvc = gv_ref[rows, cols]
            stacked = jnp.concatenate(
                [jnp.where(lane_head == j, gvc, jnp.zeros_like(gvc)) for j in range(HEADS_PER_HALF)], axis=0)
            mixed = jnp.dot(w_causal, stacked, preferred_element_type=F32) + bias
            yc_ref[0, rows, cols] = (u_ref[rows, cols] * mixed).astype(BF16)


def _odd_in(h, g, w_in, sgu_norm, sgu_w, sgu_b, fox_bf):
    B, S, D = h.shape
    T, W = OD_T, MIX_W
    n_in = 5 * W + F_PAD
    win = jnp.pad(w_in, ((0, 0), (0, n_in - w_in.shape[1]))).astype(BF16)
    seg = _block_diag(jnp.ones((N_HEADS, HEAD_DIM, HEAD_DIM), F32)).astype(BF16)
    sw = sgu_w.reshape(2, HEADS_PER_HALF, SGU_CHUNK, SGU_CHUNK).transpose(0, 2, 1, 3).reshape(
        2, SGU_CHUNK, HEADS_PER_HALF * SGU_CHUNK)
    sb = jnp.repeat(sgu_b.T, HEAD_DIM, axis=1)
    bf = jnp.pad(fox_bf, (0, F_PAD - N_HEADS)).reshape(1, F_PAD)
    tile = pl.BlockSpec((1, T, D), lambda b, t: (b, t, 0))
    half_tile = pl.BlockSpec((1, T, W), lambda b, t: (b, t, 0))
    head_tile = pl.BlockSpec((1, N_HEADS, T, PAIR_W), lambda b, t: (b, 0, t, 0))
    act = jax.ShapeDtypeStruct((B, S, W), BF16)
    per_head = jax.ShapeDtypeStruct((B, N_HEADS, S, PAIR_W), BF16)
    return pl.pallas_call(
        _odd_in_kernel,
        out_shape=(act, per_head, per_head, per_head, jax.ShapeDtypeStruct((B, S // T, 8, PAIR_W), F32)),
        grid=(B, S // T),
        in_specs=[tile, _resident((1, D)), _resident((D, n_in)), _resident((1, W)), _resident((W, W)),
                  _resident((2, SGU_CHUNK, HEADS_PER_HALF * SGU_CHUNK)), _resident((SGU_CHUNK, W)),
                  _resident((1, F_PAD))],
        out_specs=(half_tile, head_tile, head_tile, head_tile,
                   pl.BlockSpec((1, 1, 8, PAIR_W), lambda b, t: (b, t, 0, 0))),
        scratch_shapes=[pltpu.VMEM((T, D), BF16),
                        pltpu.VMEM((T, W), BF16),
                        pltpu.VMEM((T, W), F32),
                        pltpu.VMEM((N_HEADS, 128), F32)],
        compiler_params=pltpu.CompilerParams(dimension_semantics=("arbitrary", "arbitrary"),
                                             vmem_limit_bytes=VMEM_LIMIT_BYTES),
        name="odd_in",
    )(h, g.reshape(1, D), win, sgu_norm.reshape(1, W), seg, sw, sb, bf)


AT_T = 512
M_INIT = -1e30
QK_AHEAD = 2
SKIP_LOG2 = -160.0
NORM_SLACK = 1.05


def _first_needed_block(stats):
    c_first = stats[:, :, 0, :N_HEADS]
    c_last = stats[:, :, 1, :N_HEADS]
    q_norm = stats[:, :, 2, 0]
    k_norm = jnp.max(stats[:, :, 3, 0], axis=1, keepdims=True)
    spread = NORM_SLACK * 2.0 * q_norm * k_norm
    decay = (c_last[:, None, :, :] - c_first[:, :, None, :]) * LOG2E
    n = stats.shape[1]
    block = lax.broadcasted_iota(jnp.int32, (n, n), 1)
    tile = lax.broadcasted_iota(jnp.int32, (n, n), 0)
    worst = jnp.max(spread[:, :, None, None] - decay, axis=-1)
    needed = (worst >= SKIP_LOG2) | (block >= tile)[None]
    return jnp.min(jnp.where(needed, block[None], n), axis=-1).astype(jnp.int32)


def _attn_kernel(first_ref, q_ref, k_ref, v_ref, yc_ref, h_ref, wout_ref, o_ref,
                 s_ref, mx_ref, m_ref, acc_ref, yd_ref):
    T, W = AT_T, MIX_W
    qi = pl.program_id(1)
    m_ref[...] = jnp.full(m_ref.shape, M_INIT, F32)
    acc_ref[...] = jnp.zeros(acc_ref.shape, F32)

    def block(j, diagonal):
        k0 = pl.multiple_of(j * T, T)
        if diagonal:
            visible = (lax.broadcasted_iota(jnp.int32, (T, T), 1) <= lax.broadcasted_iota(jnp.int32, (T, T), 0))
        def logits(hh):
            s = lax.dot_general(q_ref[0, hh], k_ref[0, hh, pl.ds(k0, T), :], (((1,), (1,)), ((), ())),
                                preferred_element_type=F32)
            if diagonal:
                s = jnp.where(visible, s, -jnp.inf)
            s_ref[hh] = s
            mx_ref[hh] = jnp.broadcast_to(jnp.max(s, axis=-1, keepdims=True), (T, PAIR_W))

        for hh in range(QK_AHEAD):
            logits(hh)
        for hh in range(N_HEADS):
            if hh + QK_AHEAD < N_HEADS:
                logits(hh + QK_AHEAD)
            m_old = m_ref[hh]
            m_new = jnp.maximum(m_old, mx_ref[hh])
            alpha = jnp.exp2(m_old - m_new)
            p = jnp.exp2(s_ref[hh] - jnp.concatenate([m_new] * (T // PAIR_W), axis=1))
            m_ref[hh] = m_new
            acc_ref[hh] = alpha * acc_ref[hh] + jnp.dot(p.astype(BF16), v_ref[0, hh, pl.ds(k0, T), :],
                                                        preferred_element_type=F32)

    def off_diagonal(j, carry):
        block(j, False)
        return carry

    lax.fori_loop(first_ref[pl.program_id(0), qi], qi, off_diagonal, 0)
    block(qi, True)

    low = lax.broadcasted_iota(jnp.int32, (T, PAIR_W), 1) < HEAD_DIM
    for pair in range(N_PAIRS):
        acc_a = acc_ref[2 * pair]
        acc_b = acc_ref[2 * pair + 1]
        y = jnp.where(low, acc_a / acc_a[:, HEAD_DIM:HEAD_DIM + 1], acc_b / acc_b[:, 0:1])
        yd_ref[:, pair * PAIR_W:(pair + 1) * PAIR_W] = y.astype(BF16)
    o_ref[0] = (h_ref[0]
                + jnp.dot(yc_ref[0], wout_ref[0:W, :], preferred_element_type=F32)
                + jnp.dot(yd_ref[...], wout_ref[W:2 * W, :], preferred_element_type=F32))


def _attention(q, k, v, stats, yc, h, w_out):
    B, S, D = h.shape
    T, W = AT_T, MIX_W
    assert OD_T == AT_T, "the tile summaries are per odd_in tile"
    q_tile = pl.BlockSpec((1, N_HEADS, T, PAIR_W), lambda b, t: (b, 0, t, 0))
    seq = pl.BlockSpec((1, N_HEADS, S, PAIR_W), lambda b, t: (b, 0, 0, 0), pipeline_mode=pl.Buffered(1))
    tile = pl.BlockSpec((1, T, D), lambda b, t: (b, t, 0))
    return pl.pallas_call(
        _attn_kernel,
        out_shape=jax.ShapeDtypeStruct((B, S, D), F32),
        grid=(B, S // T),
        in_specs=[pl.BlockSpec(memory_space=pltpu.SMEM), q_tile, seq, seq,
                  pl.BlockSpec((1, T, W), lambda b, t: (b, t, 0)), tile, _resident((2 * W, D))],
        out_specs=tile,
        scratch_shapes=[pltpu.VMEM((N_HEADS, T, T), F32),
                        pltpu.VMEM((N_HEADS, T, PAIR_W), F32),
                        pltpu.VMEM((N_HEADS, T, PAIR_W), F32),
                        pltpu.VMEM((N_HEADS, T, PAIR_W), F32),
                        pltpu.VMEM((T, W), BF16)],
        compiler_params=pltpu.CompilerParams(dimension_semantics=("arbitrary", "arbitrary"),
                                             vmem_limit_bytes=VMEM_LIMIT_BYTES),
        name="fox_attention",
    )(_first_needed_block(stats), q, k, v, yc, h, w_out.astype(BF16))


def kernel(x, mix0_norm, mix0_w_in, lru_conv_w, lru_conv_b, lru_wa, lru_ba, lru_wx, lru_bx, lru_lambda,
           sconv_w, sconv_b, mix0_w_out, mix1_norm, mix1_w_in, sgu_norm, sgu_w, sgu_b, fox_bf, mix1_w_out,
           ffn_norm, ffn_up, ffn_conv_w, ffn_conv_b, ffn_down, final_norm):
    h = x
    h = _even_mixer(h, mix0_norm[0], mix0_w_in[0], lru_conv_w[0], lru_conv_b[0], lru_wa[0], lru_ba[0],
                    lru_wx[0], lru_bx[0], lru_lambda[0], sconv_w[0], sconv_b[0], mix0_w_out[0])
    h = _ffn(h, ffn_norm[0], ffn_up[0], ffn_conv_w[0], ffn_conv_b[0], ffn_down[0], final_norm, final_norm=False)
    yc, q, k, v, stats = _odd_in(h, mix1_norm[0], mix1_w_in[0], sgu_norm[0], sgu_w[0], sgu_b[0], fox_bf[0])
    h = _attention(q, k, v, stats, yc, h, mix1_w_out[0])
    h = _ffn(h, ffn_norm[1], ffn_up[1], ffn_conv_w[1], ffn_conv_b[1], ffn_down[1], final_norm, final_norm=True)
    return h
```

```python
import functools

import jax
import jax.numpy as jnp
from jax import lax
from jax.experimental import pallas as pl
from jax.experimental.pallas import tpu as pltpu

F32 = jnp.float32
BF16 = jnp.bfloat16

EPS = 1e-6
D_MODEL = 1024
D_FF = 2816
FFN_TAPS = 3
HALO = 8
VMEM_LIMIT_BYTES = 56 * 1024 * 1024


def _rmsnorm(x, g):
    ms = jnp.mean(x * x, axis=-1, keepdims=True)
    return x * lax.rsqrt(ms + EPS) * g


def _sigmoid(x):
    return 1.0 / (1.0 + jnp.exp(-x))


def _resident(shape):
    return pl.BlockSpec(shape, lambda *_: (0,) * len(shape), pipeline_mode=pl.Buffered(1))


def _layer_block(shape, layer):
    return pl.BlockSpec((None,) + tuple(shape), lambda *_: (layer,) + (0,) * len(shape),
                        pipeline_mode=pl.Buffered(1))


def _rows(p):
    return p[:, None, :]


FFN_T = 512
FFN_CH = 256
FFN_NCH = D_FF // FFN_CH
SUBLANES = 8
FFN_PAD = (FFN_TAPS - 1) * SUBLANES


def _ffn_kernel(h_hbm, g_ref, wup_ref, cw_ref, cb_ref, wdn_ref, fg_ref, o_hbm,
                hbuf_ref, obuf_ref, xn_ref, work0_ref, work1_ref, act0_ref, act1_ref, halo_ref, acc_ref,
                sem_in, sem_out, *, final_norm, n_tiles, n_steps):
    T, CH, PAD = FFN_T, FFN_CH, FFN_PAD
    G = T // SUBLANES
    b = pl.program_id(0)
    t = pl.program_id(1)
    n = b * n_tiles + t
    slot = lax.rem(n, 2)

    def in_copies(bb, tt, sl):
        return [pltpu.make_async_copy(h_hbm.at[bb, pl.ds(tt * T + r * G, G), :], hbuf_ref.at[sl, :, r, :],
                                      sem_in.at[sl]) for r in range(SUBLANES)]

    def out_copies(bb, tt, sl):
        return [pltpu.make_async_copy(obuf_ref.at[sl, :, r, :], o_hbm.at[bb, pl.ds(tt * T + r * G, G), :],
                                      sem_out.at[sl]) for r in range(SUBLANES)]

    @pl.when(n == 0)
    def _():
        for c in in_copies(b, t, slot):
            c.start()

    @pl.when(n + 1 < n_steps)
    def _():
        n1 = n + 1
        for c in in_copies(n1 // n_tiles, lax.rem(n1, n_tiles), 1 - slot):
            c.start()

    @pl.when(t == 0)
    def _():
        halo_ref[...] = jnp.zeros(halo_ref.shape, F32)

    for c in in_copies(b, t, slot):
        c.wait()
    h = hbuf_ref[slot].reshape(T, D_MODEL)
    xn_ref[...] = _rmsnorm(h, g_ref[...]).astype(BF16)
    acc_ref[...] = h

    works = (work0_ref, work1_ref)
    acts = (act0_ref, act1_ref)
    first_row = lax.broadcasted_iota(jnp.int32, (SUBLANES, CH), 0) == 0

    def chunk_cols(j):
        return slice(j * CH, (j + 1) * CH), slice(D_FF + j * CH, D_FF + (j + 1) * CH)

    def up_proj_half(j, half):
        work_ref = works[j % 2]
        hcols = slice(half * CH, (half + 1) * CH)
        work_ref[PAD:PAD + T, hcols] = jnp.dot(xn_ref[...], wup_ref[:, chunk_cols(j)[half]],
                                               preferred_element_type=F32)
        for k in range(FFN_TAPS - 1):
            rows = slice(k * SUBLANES, (k + 1) * SUBLANES)
            tail = pltpu.roll(work_ref[T + k * SUBLANES:T + (k + 1) * SUBLANES, hcols], 1, axis=0)
            work_ref[rows, hcols] = jnp.where(first_row, halo_ref[j, rows, hcols], tail)
            halo_ref[j, rows, hcols] = tail

    def down_proj_half(j, half):
        cols = slice(half * (D_MODEL // 2), (half + 1) * (D_MODEL // 2))
        acc_ref[:, cols] += jnp.dot(acts[j % 2][...], wdn_ref[j * CH:(j + 1) * CH, cols],
                                    preferred_element_type=F32)

    up_proj_half(0, 0)
    up_proj_half(0, 1)
    for j in range(FFN_NCH):
        for half in range(2):
            if j + 1 < FFN_NCH:
                up_proj_half(j + 1, half)
            if j >= 1:
                down_proj_half(j - 1, half)
        work_ref = works[j % 2]
        gate_cols, val_cols = chunk_cols(j)
        cw = jnp.concatenate([cw_ref[:, gate_cols], cw_ref[:, val_cols]], axis=1)
        y = jnp.concatenate([cb_ref[:, gate_cols], cb_ref[:, val_cols]], axis=1)
        for k in range(FFN_TAPS):
            y = y + work_ref[k * SUBLANES:k * SUBLANES + T, :] * cw[k:k + 1, :]
        gate = y[:, :CH]
        val = y[:, CH:]
        acts[j % 2][...] = (gate * _sigmoid(gate) * val).astype(BF16)
    down_proj_half(FFN_NCH - 1, 0)
    down_proj_half(FFN_NCH - 1, 1)

    out = acc_ref[...]
    if final_norm:
        out = _rmsnorm(out, fg_ref[...])

    @pl.when(n >= 2)
    def _():
        for c in out_copies(b, t, slot):
            c.wait()

    obuf_ref[slot] = out.reshape(G, SUBLANES, D_MODEL)
    for c in out_copies(b, t, slot):
        c.start()

    @pl.when(n == n_steps - 1)
    def _():
        for c in out_copies(b, t, slot):
            c.wait()
        if n_steps >= 2:
            for c in out_copies(b, t, 1 - slot):
                c.wait()


def _ffn(h, layer, g, w_up, conv_w, conv_b, w_down, final_g, *, final_norm):
    B, S, D = h.shape
    T, CH, NCH = FFN_T, FFN_CH, FFN_NCH
    G = T // SUBLANES
    n_tiles = S // T
    return pl.pallas_call(
        functools.partial(_ffn_kernel, final_norm=final_norm, n_tiles=n_tiles, n_steps=B * n_tiles),
        out_shape=jax.ShapeDtypeStruct((B, S, D), F32),
        grid=(B, n_tiles),
        in_specs=[pl.BlockSpec(memory_space=pl.ANY), _layer_block((1, D), layer),
                  _layer_block((D, 2 * D_FF), layer), _layer_block((FFN_TAPS, 2 * D_FF), layer),
                  _layer_block((1, 2 * D_FF), layer), _layer_block((D_FF, D), layer), _resident((1, D))],
        out_specs=pl.BlockSpec(memory_space=pl.ANY),
        scratch_shapes=[pltpu.VMEM((2, G, SUBLANES, D), F32),
                        pltpu.VMEM((2, G, SUBLANES, D), F32),
                        pltpu.VMEM((T, D), BF16),
                        pltpu.VMEM((T + FFN_PAD, 2 * CH), F32),
                        pltpu.VMEM((T + FFN_PAD, 2 * CH), F32),
                        pltpu.VMEM((T, CH), BF16),
                        pltpu.VMEM((T, CH), BF16),
                        pltpu.VMEM((NCH, FFN_PAD, 2 * CH), F32),
                        pltpu.VMEM((T, D), F32),
                        pltpu.SemaphoreType.DMA((2,)),
                        pltpu.SemaphoreType.DMA((2,))],
        compiler_params=pltpu.CompilerParams(dimension_semantics=("arbitrary", "arbitrary"),
                                             vmem_limit_bytes=VMEM_LIMIT_BYTES),
        name="ffn_final" if final_norm else "ffn",
    )(h, _rows(g), w_up, conv_w, _rows(conv_b), w_down, final_g.reshape(1, D))


EV_T = 512
MIX_W = 512
LRU_TAPS = 4
SC_TAPS = 3
LRU_DECAY_C = 8.0
GELU_C = 0.7978845608028654


def _gelu(x):
    return x * (0.5 * (1.0 + jnp.tanh(GELU_C * (x + 0.044715 * (x * x * x)))))


def _causal_taps(buf_ref, w_ref, b_ref, taps, T):
    w = w_ref[...]
    y = b_ref[...]
    for k in range(taps):
        off = HALO - (taps - 1) + k
        y = y + buf_ref[off:off + T, :] * w[k:k + 1, :]
    return y


def _even_kernel(h_ref, g_ref, win_ref, lcw_ref, lcb_ref, wg_ref, bg_ref, lam_ref, scw_ref, scb_ref, wout_ref,
                 o_ref, xn_ref, xa_ref, cv_ref, a_ref, hs_ref, y_ref, state_ref):
    T, W = EV_T, MIX_W

    @pl.when(pl.program_id(1) == 0)
    def _():
        xa_ref[0:HALO, :] = jnp.zeros((HALO, W), F32)
        cv_ref[0:HALO, :] = jnp.zeros((HALO, W), F32)
        state_ref[...] = jnp.zeros(state_ref.shape, F32)

    h = h_ref[0]
    xn_ref[...] = _rmsnorm(h, g_ref[...]).astype(BF16)

    def proj(c):
        return jnp.dot(xn_ref[...], win_ref[:, c * W:(c + 1) * W], preferred_element_type=F32)

    xa_ref[HALO:HALO + T, :] = proj(0)
    xc = _causal_taps(xa_ref, lcw_ref, lcb_ref, LRU_TAPS, T)
    xa_ref[0:HALO, :] = xa_ref[T:T + HALO, :]
    pre = jnp.dot(xc.astype(BF16), wg_ref[...], preferred_element_type=F32) + bg_ref[...]
    r = _sigmoid(pre[:, :W])
    i = _sigmoid(pre[:, W:])
    lam = lam_ref[...]
    softplus_neg_lam = jnp.maximum(-lam, 0.0) + jnp.log1p(jnp.exp(-jnp.abs(lam)))
    log_a = (-LRU_DECAY_C * r) * softplus_neg_lam
    a = jnp.exp(log_a)
    th = jnp.tanh(log_a)
    u = jnp.sqrt((-2.0 * th) / (1.0 - th)) * (i * xc)

    G = T // 8
    a3 = a.reshape(G, 8, W)
    u3 = u.reshape(G, 8, W)
    row = lax.broadcasted_iota(jnp.int32, (G, 8, W), 1)
    for d in (1, 2, 4):
        keep = row >= d
        u3 = u3 + a3 * jnp.where(keep, pltpu.roll(u3, d, axis=1), 0.0)
        a3 = a3 * jnp.where(keep, pltpu.roll(a3, d, axis=1), 1.0)
    a_ref[...] = a3.reshape(T, W)
    hs_ref[...] = u3.reshape(T, W)
    carry = state_ref[0:1, :]
    for gi in range(G):
        rows = slice(gi * 8, gi * 8 + 8)
        hg = hs_ref[rows, :] + a_ref[rows, :] * carry
        hs_ref[rows, :] = hg
        carry = hg[7:8, :]
    state_ref[0:1, :] = carry
    y_ref[:, 0:W] = (hs_ref[...] * _gelu(proj(1))).astype(BF16)

    cv_ref[HALO:HALO + T, :] = proj(2) * proj(4)
    cc = _causal_taps(cv_ref, scw_ref, scb_ref, SC_TAPS, T)
    cv_ref[0:HALO, :] = cv_ref[T:T + HALO, :]
    y_ref[:, W:2 * W] = (proj(3) * cc).astype(BF16)

    o_ref[0] = h + jnp.dot(y_ref[...], wout_ref[...], preferred_element_type=F32)


def _block_diag(w):
    H, d, _ = w.shape
    return jnp.einsum('hij,hg->higj', w, jnp.eye(H, dtype=w.dtype)).reshape(H * d, H * d)


def _even_mixer(h, layer, g, w_in, lcw, lcb, wa, ba, wx, bx, lam, scw, scb, w_out):
    B, S, D = h.shape
    T, W = EV_T, MIX_W
    wg = jnp.concatenate([_block_diag(wa[layer]), _block_diag(wx[layer])], axis=1).astype(BF16)
    bg = jnp.concatenate([ba[layer], bx[layer]]).reshape(1, 2 * W)
    tile = pl.BlockSpec((1, T, D), lambda b, t: (b, t, 0))
    return pl.pallas_call(
        _even_kernel,
        out_shape=jax.ShapeDtypeStruct((B, S, D), F32),
        grid=(B, S // T),
        in_specs=[tile, _layer_block((1, D), layer), _layer_block((D, 5 * W), layer),
                  _layer_block((LRU_TAPS, W), layer), _layer_block((1, W), layer),
                  _resident((W, 2 * W)), _resident((1, 2 * W)), _layer_block((1, W), layer),
                  _layer_block((SC_TAPS, W), layer), _layer_block((1, W), layer), _layer_block((2 * W, D), layer)],
        out_specs=tile,
        scratch_shapes=[pltpu.VMEM((T, D), BF16),
                        pltpu.VMEM((T + HALO, W), F32),
                        pltpu.VMEM((T + HALO, W), F32),
                        pltpu.VMEM((T, W), F32),
                        pltpu.VMEM((T, W), F32),
                        pltpu.VMEM((T, 2 * W), BF16),
                        pltpu.VMEM((8, W), F32)],
        compiler_params=pltpu.CompilerParams(dimension_semantics=("arbitrary", "arbitrary"),
                                             vmem_limit_bytes=VMEM_LIMIT_BYTES),
        name="even_mixer",
    )(h, _rows(g), w_in, lcw, _rows(lcb), wg, bg, _rows(lam), scw, _rows(scb), w_out)


OD_T = 512
SGU_CHUNK = 128
N_HEADS = 8
HEAD_DIM = 64
F_PAD = 128
HEADS_PER_HALF = 4
QK_SCALE = HEAD_DIM ** -0.5
LOG2E = 1.4426950408889634
PAIR_W = 2 * HEAD_DIM
N_PAIRS = N_HEADS // 2


def _split3_bf16(x):
    hi = x.astype(BF16)
    r1 = x - hi.astype(F32)
    mid = r1.astype(BF16)
    lo = (r1 - mid.astype(F32)).astype(BF16)
    return hi, mid, lo


def _odd_in_kernel(h_ref, g_ref, win_ref, gn_ref, seg_ref, sw_ref, sb_ref, bf_ref,
                   yc_ref, q_ref, k_ref, v_ref, st_ref, xn_ref, gv_ref, u_ref, csum_ref):
    T, W = OD_T, MIX_W

    @pl.when(pl.program_id(1) == 0)
    def _():
        csum_ref[...] = jnp.zeros(csum_ref.shape, F32)

    xn_ref[...] = _rmsnorm(h_ref[0], g_ref[...]).astype(BF16)

    def proj(lo, width):
        return jnp.dot(xn_ref[...], win_ref[:, lo:lo + width], preferred_element_type=F32)

    fl = proj(5 * W, F_PAD) + bf_ref[...]
    logf = jnp.minimum(fl, 0.0) - jnp.log1p(jnp.exp(-jnp.abs(fl)))
    tri = (lax.broadcasted_iota(jnp.int32, (T, T), 1) <= lax.broadcasted_iota(jnp.int32, (T, T), 0)).astype(BF16)
    c_loc = sum(jnp.dot(tri, part, preferred_element_type=F32) for part in _split3_bf16(logf))
    c_tile = c_loc + csum_ref[0:1, :]
    csum_ref[0:1, :] = c_tile[T - 1:T, :]

    lane = lax.broadcasted_iota(jnp.int32, (T, PAIR_W), 1)
    low = lane < HEAD_DIM
    pq = proj(2 * W, W) * (QK_SCALE * LOG2E)
    pk = proj(3 * W, W)
    pv = proj(4 * W, W)
    for hh in range(N_HEADS):
        blk = slice((hh // 2) * PAIR_W, (hh // 2 + 1) * PAIR_W)
        own = low if hh % 2 == 0 else jnp.logical_not(low)
        a0 = HEAD_DIM if hh % 2 == 0 else 0
        x = jnp.broadcast_to(c_tile[:, hh:hh + 1] * (-LOG2E), (T, PAIR_W))
        hi = x.astype(BF16).astype(F32)
        r1 = x - hi
        mid = r1.astype(BF16).astype(F32)
        lo = r1 - mid
        c_parts = jnp.where(lane == a0, hi, jnp.where(lane == a0 + 1, mid, jnp.where(lane == a0 + 2, lo, 0.0)))
        ones3 = jnp.where((lane >= a0) & (lane < a0 + 3), 1.0, 0.0)
        one1 = jnp.where(lane == a0, 1.0, 0.0)
        q_ref[0, hh] = jnp.where(own, pq[:, blk], ones3).astype(BF16)
        k_ref[0, hh] = jnp.where(own, pk[:, blk], c_parts).astype(BF16)
        v_ref[0, hh] = jnp.where(own, pv[:, blk], one1).astype(BF16)

    def max_block_norm(x):
        sq = [jnp.max(jnp.sum(x[:, p * PAIR_W:(p + 1) * PAIR_W] ** 2, axis=-1, keepdims=True), axis=0, keepdims=True)
              for p in range(N_PAIRS)]
        return jnp.broadcast_to(jnp.sqrt(functools.reduce(jnp.maximum, sq)), (1, PAIR_W))

    st_ref[0, 0] = jnp.concatenate([c_tile[0:1, :], c_tile[T - 1:T, :], max_block_norm(pq), max_block_norm(pk),
                                    c_tile[T // 2 - 1:T // 2, :], jnp.zeros((3, PAIR_W), F32)], axis=0)

    u_ref[...] = _gelu(proj(0, W))
    g = _gelu(proj(W, W))
    gsq = g * g
    gsq_hi = gsq.astype(BF16)
    gsq_lo = (gsq - gsq_hi.astype(F32)).astype(BF16)
    ms = (jnp.dot(gsq_hi, seg_ref[...], preferred_element_type=F32)
          + jnp.dot(gsq_lo, seg_ref[...], preferred_element_type=F32)) * (1.0 / HEAD_DIM)
    gv_ref[...] = (g * lax.rsqrt(ms + EPS) * gn_ref[...]).astype(BF16)

    HW = HEADS_PER_HALF * HEAD_DIM
    lane_head = lax.broadcasted_iota(jnp.int32, (SGU_CHUNK, HW), 1) // HEAD_DIM
    wrow = lax.broadcasted_iota(jnp.int32, (SGU_CHUNK, HEADS_PER_HALF * SGU_CHUNK), 0)
    wcol = lax.broadcasted_iota(jnp.int32, (SGU_CHUNK, HEADS_PER_HALF * SGU_CHUNK), 1) % SGU_CHUNK
    for half in range(2):
        cols = slice(half * HW, (half + 1) * HW)
        w_causal = jnp.where(wcol <= wrow, sw_ref[half], 0.0).astype(BF16)
        bias = sb_ref[:, cols]
        for c in range(T // SGU_CHUNK):
            rows = slice(c * SGU_CHUNK, (c + 1) * SGU_CHUNK)
            gvc = gv_ref[rows, cols]
            stacked = jnp.concatenate(
                [jnp.where(lane_head == j, gvc, jnp.zeros_like(gvc)) for j in range(HEADS_PER_HALF)], axis=0)
            mixed = jnp.dot(w_causal, stacked, preferred_element_type=F32) + bias
            yc_ref[0, rows, cols] = (u_ref[rows, cols] * mixed).astype(BF16)


def _odd_in(h, layer, g, w_in, sgu_norm, sgu_w, sgu_b, fox_bf):
    B, S, D = h.shape
    T, W = OD_T, MIX_W
    n_in = 5 * W + F_PAD
    win = jnp.pad(w_in, ((0, 0), (0, 0), (0, n_in - w_in.shape[2]))).astype(BF16)
    seg = _block_diag(jnp.ones((N_HEADS, HEAD_DIM, HEAD_DIM), F32)).astype(BF16)
    sw = sgu_w[layer].reshape(2, HEADS_PER_HALF, SGU_CHUNK, SGU_CHUNK).transpose(0, 2, 1, 3).reshape(
        2, SGU_CHUNK, HEADS_PER_HALF * SGU_CHUNK)
    sb = jnp.repeat(sgu_b[layer].T, HEAD_DIM, axis=1)
    bf = jnp.pad(fox_bf[layer], (0, F_PAD - N_HEADS)).reshape(1, F_PAD)
    tile = pl.BlockSpec((1, T, D), lambda b, t: (b, t, 0))
    half_tile = pl.BlockSpec((1, T, W), lambda b, t: (b, t, 0))
    head_tile = pl.BlockSpec((1, N_HEADS, T, PAIR_W), lambda b, t: (b, 0, t, 0))
    act = jax.ShapeDtypeStruct((B, S, W), BF16)
    per_head = jax.ShapeDtypeStruct((B, N_HEADS, S, PAIR_W), BF16)
    return pl.pallas_call(
        _odd_in_kernel,
        out_shape=(act, per_head, per_head, per_head, jax.ShapeDtypeStruct((B, S // T, 8, PAIR_W), F32)),
        grid=(B, S // T),
        in_specs=[tile, _layer_block((1, D), layer), _layer_block((D, n_in), layer), _layer_block((1, W), layer),
                  _resident((W, W)), _resident((2, SGU_CHUNK, HEADS_PER_HALF * SGU_CHUNK)),
                  _resident((SGU_CHUNK, W)), _resident((1, F_PAD))],
        out_specs=(half_tile, head_tile, head_tile, head_tile,
                   pl.BlockSpec((1, 1, 8, PAIR_W), lambda b, t: (b, t, 0, 0))),
        scratch_shapes=[pltpu.VMEM((T, D), BF16),
                        pltpu.VMEM((T, W), BF16),
                        pltpu.VMEM((T, W), F32),
                        pltpu.VMEM((N_HEADS, 128), F32)],
        compiler_params=pltpu.CompilerParams(dimension_semantics=("arbitrary", "arbitrary"),
                                             vmem_limit_bytes=VMEM_LIMIT_BYTES),
        name="odd_in",
    )(h, _rows(g), win, _rows(sgu_norm), seg, sw, sb, bf)


AT_T = 512
AT_K = 256
KEY_BLOCKS_PER_TILE = AT_T // AT_K
M_INIT = -1e30
QK_AHEAD = 2
SKIP_LOG2 = -160.0
NORM_SLACK = 1.05


def _first_needed_block(stats):
    KB = KEY_BLOCKS_PER_TILE
    batch, n = stats.shape[:2]
    c_first = stats[:, :, 0, :N_HEADS]
    c_end = jnp.stack([stats[:, :, 4, :N_HEADS], stats[:, :, 1, :N_HEADS]], axis=2).reshape(batch, KB * n, N_HEADS)
    q_norm = stats[:, :, 2, 0]
    k_norm = jnp.max(stats[:, :, 3, 0], axis=1, keepdims=True)
    spread = NORM_SLACK * 2.0 * q_norm * k_norm
    decay = (c_end[:, None, :, :] - c_first[:, :, None, :]) * LOG2E
    block = lax.broadcasted_iota(jnp.int32, (n, KB * n), 1)
    tile = lax.broadcasted_iota(jnp.int32, (n, KB * n), 0)
    worst = jnp.max(spread[:, :, None, None] - decay, axis=-1)
    needed = (worst >= SKIP_LOG2) | (block >= KB * tile)[None]
    return jnp.min(jnp.where(needed, block[None], KB * n), axis=-1).astype(jnp.int32)


def _attn_kernel(first_ref, q_ref, k_ref, v_ref, yc_ref, h_ref, wout_ref, o_ref,
                 s_ref, mx_ref, m_ref, acc_ref, yd_ref):
    T, K, W, KB = AT_T, AT_K, MIX_W, KEY_BLOCKS_PER_TILE
    qi = pl.program_id(1)
    m_ref[...] = jnp.full(m_ref.shape, M_INIT, F32)
    acc_ref[...] = jnp.zeros(acc_ref.shape, F32)

    def block(kb, r0, diagonal):
        rows = slice(r0, T)
        k0 = pl.multiple_of(kb * K, K)
        if diagonal:
            visible = (lax.broadcasted_iota(jnp.int32, (T - r0, K), 1)
                       <= lax.broadcasted_iota(jnp.int32, (T - r0, K), 0))

        def logits(hh):
            s = lax.dot_general(q_ref[0, hh, rows, :], k_ref[0, hh, pl.ds(k0, K), :], (((1,), (1,)), ((), ())),
                                preferred_element_type=F32)
            if diagonal:
                s = jnp.where(visible, s, -jnp.inf)
            s_ref[hh, rows, :] = s
            mx_ref[hh, rows, :] = jnp.broadcast_to(jnp.max(s, axis=-1, keepdims=True), (T - r0, PAIR_W))

        for hh in range(QK_AHEAD):
            logits(hh)
        for hh in range(N_HEADS):
            if hh + QK_AHEAD < N_HEADS:
                logits(hh + QK_AHEAD)
            m_old = m_ref[hh, rows, :]
            m_new = jnp.maximum(m_old, mx_ref[hh, rows, :])
            alpha = jnp.exp2(m_old - m_new)
            p = jnp.exp2(s_ref[hh, rows, :] - jnp.concatenate([m_new] * (K // PAIR_W), axis=1))
            m_ref[hh, rows, :] = m_new
            acc_ref[hh, rows, :] = alpha * acc_ref[hh, rows, :] + jnp.dot(
                p.astype(BF16), v_ref[0, hh, pl.ds(k0, K), :], preferred_element_type=F32)

    def off_diagonal(kb, carry):
        block(kb, 0, False)
        return carry

    lax.fori_loop(first_ref[pl.program_id(0), qi], KB * qi, off_diagonal, 0)
    for d in range(KB):
        block(KB * qi + d, d * K, True)

    low = lax.broadcasted_iota(jnp.int32, (T, PAIR_W), 1) < HEAD_DIM
    for pair in range(N_PAIRS):
        acc_a = acc_ref[2 * pair]
        acc_b = acc_ref[2 * pair + 1]
        y = jnp.where(low, acc_a / acc_a[:, HEAD_DIM:HEAD_DIM + 1], acc_b / acc_b[:, 0:1])
        yd_ref[:, pair * PAIR_W:(pair + 1) * PAIR_W] = y.astype(BF16)
    o_ref[0] = (h_ref[0]
                + jnp.dot(yc_ref[0], wout_ref[0:W, :], preferred_element_type=F32)
                + jnp.dot(yd_ref[...], wout_ref[W:2 * W, :], preferred_element_type=F32))


def _attention(q, k, v, stats, yc, h, layer, w_out):
    B, S, D = h.shape
    T, K, W = AT_T, AT_K, MIX_W
    assert OD_T == AT_T and KEY_BLOCKS_PER_TILE == 2, "tile summaries hold c at the middle and end of an odd_in tile"
    q_tile = pl.BlockSpec((1, N_HEADS, T, PAIR_W), lambda b, t: (b, 0, t, 0))
    seq = pl.BlockSpec((1, N_HEADS, S, PAIR_W), lambda b, t: (b, 0, 0, 0), pipeline_mode=pl.Buffered(1))
    tile = pl.BlockSpec((1, T, D), lambda b, t: (b, t, 0))
    return pl.pallas_call(
        _attn_kernel,
        out_shape=jax.ShapeDtypeStruct((B, S, D), F32),
        grid=(B, S // T),
        in_specs=[pl.BlockSpec(memory_space=pltpu.SMEM), q_tile, seq, seq,
                  pl.BlockSpec((1, T, W), lambda b, t: (b, t, 0)), tile, _layer_block((2 * W, D), layer)],
        out_specs=tile,
        scratch_shapes=[pltpu.VMEM((N_HEADS, T, K), F32),
                        pltpu.VMEM((N_HEADS, T, PAIR_W), F32),
                        pltpu.VMEM((N_HEADS, T, PAIR_W), F32),
                        pltpu.VMEM((N_HEADS, T, PAIR_W), F32),
                        pltpu.VMEM((T, W), BF16)],
        compiler_params=pltpu.CompilerParams(dimension_semantics=("arbitrary", "arbitrary"),
                                             vmem_limit_bytes=VMEM_LIMIT_BYTES),
        name="fox_attention",
    )(_first_needed_block(stats), q, k, v, yc, h, w_out)


def kernel(x, mix0_norm, mix0_w_in, lru_conv_w, lru_conv_b, lru_wa, lru_ba, lru_wx, lru_bx, lru_lambda,
           sconv_w, sconv_b, mix0_w_out, mix1_norm, mix1_w_in, sgu_norm, sgu_w, sgu_b, fox_bf, mix1_w_out,
           ffn_norm, ffn_up, ffn_conv_w, ffn_conv_b, ffn_down, final_norm):
    ffn_up_b = ffn_up.astype(BF16)
    ffn_down_b = ffn_down.astype(BF16)
    h = _even_mixer(x, 0, mix0_norm, mix0_w_in.astype(BF16), lru_conv_w, lru_conv_b, lru_wa, lru_ba, lru_wx, lru_bx,
                    lru_lambda, sconv_w, sconv_b, mix0_w_out.astype(BF16))
    h = _ffn(h, 0, ffn_norm, ffn_up_b, ffn_conv_w, ffn_conv_b, ffn_down_b, final_norm, final_norm=False)
    yc, q, k, v, stats = _odd_in(h, 0, mix1_norm, mix1_w_in, sgu_norm, sgu_w, sgu_b, fox_bf)
    h = _attention(q, k, v, stats, yc, h, 0, mix1_w_out.astype(BF16))
    return _ffn(h, 1, ffn_norm, ffn_up_b, ffn_conv_w, ffn_conv_b, ffn_down_b, final_norm, final_norm=True)
```

```python
import functools

import jax
import jax.numpy as jnp
from jax import lax
from jax.experimental import pallas as pl
from jax.experimental.pallas import tpu as pltpu

F32 = jnp.float32
BF16 = jnp.bfloat16

EPS = 1e-6
D_MODEL = 1024
D_FF = 2816
FFN_TAPS = 3
HALO = 8
VMEM_LIMIT_BYTES = 56 * 1024 * 1024


def _rmsnorm(x, g):
    ms = jnp.mean(x * x, axis=-1, keepdims=True)
    return x * lax.rsqrt(ms + EPS) * g


def _sigmoid(x):
    return 1.0 / (1.0 + jnp.exp(-x))


def _resident(shape):
    return pl.BlockSpec(shape, lambda *_: (0,) * len(shape), pipeline_mode=pl.Buffered(1))


def _layer_block(shape, layer):
    return pl.BlockSpec((None,) + tuple(shape), lambda *_: (layer,) + (0,) * len(shape),
                        pipeline_mode=pl.Buffered(1))


def _rows(p):
    return p[:, None, :]


FFN_T = 512
FFN_CH = 256
FFN_NCH = D_FF // FFN_CH
SUBLANES = 8
FFN_PAD = (FFN_TAPS - 1) * SUBLANES


def _ffn_kernel(h_hbm, g_ref, wup_ref, cw_ref, cb_ref, wdn_ref, fg_ref, o_hbm,
                hbuf_ref, obuf_ref, xn_ref, work0_ref, work1_ref, act0_ref, act1_ref, halo_ref, acc_ref,
                sem_in, sem_out, *, final_norm, n_tiles, n_steps):
    T, CH, PAD = FFN_T, FFN_CH, FFN_PAD
    G = T // SUBLANES
    b = pl.program_id(0)
    t = pl.program_id(1)
    n = b * n_tiles + t
    slot = lax.rem(n, 2)

    def in_copies(bb, tt, sl):
        return [pltpu.make_async_copy(h_hbm.at[bb, pl.ds(tt * T + r * G, G), :], hbuf_ref.at[sl, :, r, :],
                                      sem_in.at[sl]) for r in range(SUBLANES)]

    def out_copies(bb, tt, sl):
        return [pltpu.make_async_copy(obuf_ref.at[sl, :, r, :], o_hbm.at[bb, pl.ds(tt * T + r * G, G), :],
                                      sem_out.at[sl]) for r in range(SUBLANES)]

    @pl.when(n == 0)
    def _():
        for c in in_copies(b, t, slot):
            c.start()

    @pl.when(n + 1 < n_steps)
    def _():
        n1 = n + 1
        for c in in_copies(n1 // n_tiles, lax.rem(n1, n_tiles), 1 - slot):
            c.start()

    @pl.when(t == 0)
    def _():
        halo_ref[...] = jnp.zeros(halo_ref.shape, F32)

    for c in in_copies(b, t, slot):
        c.wait()
    h = hbuf_ref[slot].reshape(T, D_MODEL)
    xn_ref[...] = _rmsnorm(h, g_ref[...]).astype(BF16)
    acc_ref[...] = h

    works = (work0_ref, work1_ref)
    acts = (act0_ref, act1_ref)
    first_row = lax.broadcasted_iota(jnp.int32, (SUBLANES, CH), 0) == 0

    def chunk_cols(j):
        return slice(j * CH, (j + 1) * CH), slice(D_FF + j * CH, D_FF + (j + 1) * CH)

    def up_proj_half(j, half):
        work_ref = works[j % 2]
        hcols = slice(half * CH, (half + 1) * CH)
        work_ref[PAD:PAD + T, hcols] = jnp.dot(xn_ref[...], wup_ref[:, chunk_cols(j)[half]],
                                               preferred_element_type=F32)
        for k in range(FFN_TAPS - 1):
            rows = slice(k * SUBLANES, (k + 1) * SUBLANES)
            tail = pltpu.roll(work_ref[T + k * SUBLANES:T + (k + 1) * SUBLANES, hcols], 1, axis=0)
            work_ref[rows, hcols] = jnp.where(first_row, halo_ref[j, rows, hcols], tail)
            halo_ref[j, rows, hcols] = tail

    def down_proj_half(j, half):
        cols = slice(half * (D_MODEL // 2), (half + 1) * (D_MODEL // 2))
        acc_ref[:, cols] += jnp.dot(acts[j % 2][...], wdn_ref[j * CH:(j + 1) * CH, cols],
                                    preferred_element_type=F32)

    up_proj_half(0, 0)
    up_proj_half(0, 1)
    for j in range(FFN_NCH):
        for half in range(2):
            if j + 1 < FFN_NCH:
                up_proj_half(j + 1, half)
            if j >= 1:
                down_proj_half(j - 1, half)
        work_ref = works[j % 2]
        gate_cols, val_cols = chunk_cols(j)
        cw = jnp.concatenate([cw_ref[:, gate_cols], cw_ref[:, val_cols]], axis=1)
        y = jnp.concatenate([cb_ref[:, gate_cols], cb_ref[:, val_cols]], axis=1)
        for k in range(FFN_TAPS):
            y = y + work_ref[k * SUBLANES:k * SUBLANES + T, :] * cw[k:k + 1, :]
        gate = y[:, :CH]
        val = y[:, CH:]
        acts[j % 2][...] = (gate * _sigmoid(gate) * val).astype(BF16)
    down_proj_half(FFN_NCH - 1, 0)
    down_proj_half(FFN_NCH - 1, 1)

    out = acc_ref[...]
    if final_norm:
        out = _rmsnorm(out, fg_ref[...])

    @pl.when(n >= 2)
    def _():
        for c in out_copies(b, t, slot):
            c.wait()

    obuf_ref[slot] = out.reshape(G, SUBLANES, D_MODEL)
    for c in out_copies(b, t, slot):
        c.start()

    @pl.when(n == n_steps - 1)
    def _():
        for c in out_copies(b, t, slot):
            c.wait()
        if n_steps >= 2:
            for c in out_copies(b, t, 1 - slot):
                c.wait()


def _ffn(h, layer, g, w_up, conv_w, conv_b, w_down, final_g, *, final_norm):
    B, S, D = h.shape
    T, CH, NCH = FFN_T, FFN_CH, FFN_NCH
    G = T // SUBLANES
    n_tiles = S // T
    return pl.pallas_call(
        functools.partial(_ffn_kernel, final_norm=final_norm, n_tiles=n_tiles, n_steps=B * n_tiles),
        out_shape=jax.ShapeDtypeStruct((B, S, D), F32),
        grid=(B, n_tiles),
        in_specs=[pl.BlockSpec(memory_space=pl.ANY), _layer_block((1, D), layer),
                  _layer_block((D, 2 * D_FF), layer), _layer_block((FFN_TAPS, 2 * D_FF), layer),
                  _layer_block((1, 2 * D_FF), layer), _layer_block((D_FF, D), layer), _resident((1, D))],
        out_specs=pl.BlockSpec(memory_space=pl.ANY),
        scratch_shapes=[pltpu.VMEM((2, G, SUBLANES, D), F32),
                        pltpu.VMEM((2, G, SUBLANES, D), F32),
                        pltpu.VMEM((T, D), BF16),
                        pltpu.VMEM((T + FFN_PAD, 2 * CH), F32),
                        pltpu.VMEM((T + FFN_PAD, 2 * CH), F32),
                        pltpu.VMEM((T, CH), BF16),
                        pltpu.VMEM((T, CH), BF16),
                        pltpu.VMEM((NCH, FFN_PAD, 2 * CH), F32),
                        pltpu.VMEM((T, D), F32),
                        pltpu.SemaphoreType.DMA((2,)),
                        pltpu.SemaphoreType.DMA((2,))],
        compiler_params=pltpu.CompilerParams(dimension_semantics=("arbitrary", "arbitrary"),
                                             vmem_limit_bytes=VMEM_LIMIT_BYTES),
        name="ffn_final" if final_norm else "ffn",
    )(h, _rows(g), w_up, conv_w, _rows(conv_b), w_down, final_g.reshape(1, D))


EV_T = 512
MIX_W = 512
LRU_TAPS = 4
SC_TAPS = 3
LRU_DECAY_C = 8.0
GELU_C = 0.7978845608028654


def _gelu(x):
    return x * (0.5 * (1.0 + jnp.tanh(GELU_C * (x + 0.044715 * (x * x * x)))))


def _causal_taps(buf_ref, w_ref, b_ref, taps, T):
    w = w_ref[...]
    y = b_ref[...]
    for k in range(taps):
        off = HALO - (taps - 1) + k
        y = y + buf_ref[off:off + T, :] * w[k:k + 1, :]
    return y


def _even_kernel(h_ref, g_ref, win_ref, lcw_ref, lcb_ref, wg_ref, bg_ref, lam_ref, scw_ref, scb_ref, wout_ref,
                 o_ref, xn_ref, xa_ref, cv_ref, a_ref, hs_ref, y_ref, state_ref):
    T, W = EV_T, MIX_W

    @pl.when(pl.program_id(1) == 0)
    def _():
        xa_ref[0:HALO, :] = jnp.zeros((HALO, W), F32)
        cv_ref[0:HALO, :] = jnp.zeros((HALO, W), F32)
        state_ref[...] = jnp.zeros(state_ref.shape, F32)

    h = h_ref[0]
    xn_ref[...] = _rmsnorm(h, g_ref[...]).astype(BF16)

    def proj(c):
        return jnp.dot(xn_ref[...], win_ref[:, c * W:(c + 1) * W], preferred_element_type=F32)

    xa_ref[HALO:HALO + T, :] = proj(0)
    xc = _causal_taps(xa_ref, lcw_ref, lcb_ref, LRU_TAPS, T)
    xa_ref[0:HALO, :] = xa_ref[T:T + HALO, :]
    pre = jnp.dot(xc.astype(BF16), wg_ref[...], preferred_element_type=F32) + bg_ref[...]
    r = _sigmoid(pre[:, :W])
    i = _sigmoid(pre[:, W:])
    lam = lam_ref[...]
    softplus_neg_lam = jnp.maximum(-lam, 0.0) + jnp.log1p(jnp.exp(-jnp.abs(lam)))
    log_a = (-LRU_DECAY_C * r) * softplus_neg_lam
    a = jnp.exp(log_a)
    th = jnp.tanh(log_a)
    u = jnp.sqrt((-2.0 * th) / (1.0 - th)) * (i * xc)

    G = T // 8
    a3 = a.reshape(G, 8, W)
    u3 = u.reshape(G, 8, W)
    row = lax.broadcasted_iota(jnp.int32, (G, 8, W), 1)
    for d in (1, 2, 4):
        keep = row >= d
        u3 = u3 + a3 * jnp.where(keep, pltpu.roll(u3, d, axis=1), 0.0)
        a3 = a3 * jnp.where(keep, pltpu.roll(a3, d, axis=1), 1.0)
    a_ref[...] = a3.reshape(T, W)
    hs_ref[...] = u3.reshape(T, W)
    carry = state_ref[0:1, :]
    for gi in range(G):
        rows = slice(gi * 8, gi * 8 + 8)
        hg = hs_ref[rows, :] + a_ref[rows, :] * carry
        hs_ref[rows, :] = hg
        carry = hg[7:8, :]
    state_ref[0:1, :] = carry
    y_ref[:, 0:W] = (hs_ref[...] * _gelu(proj(1))).astype(BF16)

    cv_ref[HALO:HALO + T, :] = proj(2) * proj(4)
    cc = _causal_taps(cv_ref, scw_ref, scb_ref, SC_TAPS, T)
    cv_ref[0:HALO, :] = cv_ref[T:T + HALO, :]
    y_ref[:, W:2 * W] = (proj(3) * cc).astype(BF16)

    o_ref[0] = h + jnp.dot(y_ref[...], wout_ref[...], preferred_element_type=F32)


def _block_diag(w):
    H, d, _ = w.shape
    return jnp.einsum('hij,hg->higj', w, jnp.eye(H, dtype=w.dtype)).reshape(H * d, H * d)


def _even_mixer(h, layer, g, w_in, lcw, lcb, wa, ba, wx, bx, lam, scw, scb, w_out):
    B, S, D = h.shape
    T, W = EV_T, MIX_W
    wg = jnp.concatenate([_block_diag(wa[layer]), _block_diag(wx[layer])], axis=1).astype(BF16)
    bg = jnp.concatenate([ba[layer], bx[layer]]).reshape(1, 2 * W)
    tile = pl.BlockSpec((1, T, D), lambda b, t: (b, t, 0))
    return pl.pallas_call(
        _even_kernel,
        out_shape=jax.ShapeDtypeStruct((B, S, D), F32),
        grid=(B, S // T),
        in_specs=[tile, _layer_block((1, D), layer), _layer_block((D, 5 * W), layer),
                  _layer_block((LRU_TAPS, W), layer), _layer_block((1, W), layer),
                  _resident((W, 2 * W)), _resident((1, 2 * W)), _layer_block((1, W), layer),
                  _layer_block((SC_TAPS, W), layer), _layer_block((1, W), layer), _layer_block((2 * W, D), layer)],
        out_specs=tile,
        scratch_shapes=[pltpu.VMEM((T, D), BF16),
                        pltpu.VMEM((T + HALO, W), F32),
                        pltpu.VMEM((T + HALO, W), F32),
                        pltpu.VMEM((T, W), F32),
                        pltpu.VMEM((T, W), F32),
                        pltpu.VMEM((T, 2 * W), BF16),
                        pltpu.VMEM((8, W), F32)],
        compiler_params=pltpu.CompilerParams(dimension_semantics=("arbitrary", "arbitrary"),
                                             vmem_limit_bytes=VMEM_LIMIT_BYTES),
        name="even_mixer",
    )(h, _rows(g), w_in, lcw, _rows(lcb), wg, bg, _rows(lam), scw, _rows(scb), w_out)


OD_T = 512
SGU_CHUNK = 128
N_HEADS = 8
HEAD_DIM = 64
F_PAD = 128
HEADS_PER_HALF = 4
QK_SCALE = HEAD_DIM ** -0.5
LOG2E = 1.4426950408889634
PAIR_W = 2 * HEAD_DIM
N_PAIRS = N_HEADS // 2


def _odd_in_kernel(h_ref, g_ref, win_ref, gn_ref, seg_ref, sw_ref, sb_ref, bf_ref,
                   yc_ref, q_ref, k_ref, v_ref, st_ref, xn_ref, gv_ref, u_ref, csum_ref):
    T, W = OD_T, MIX_W

    @pl.when(pl.program_id(1) == 0)
    def _():
        csum_ref[...] = jnp.zeros(csum_ref.shape, F32)

    xn_ref[...] = _rmsnorm(h_ref[0], g_ref[...]).astype(BF16)

    def proj(lo, width):
        return jnp.dot(xn_ref[...], win_ref[:, lo:lo + width], preferred_element_type=F32)

    fl = proj(5 * W, F_PAD) + bf_ref[...]
    logf = jnp.minimum(fl, 0.0) - jnp.log1p(jnp.exp(-jnp.abs(fl)))
    tri = (lax.broadcasted_iota(jnp.int32, (T, T), 1) <= lax.broadcasted_iota(jnp.int32, (T, T), 0)).astype(BF16)
    logf_hi = logf.astype(BF16)
    logf_mid = (logf - logf_hi.astype(F32)).astype(BF16)
    c_parts = jnp.dot(tri, jnp.concatenate([logf_hi, logf_mid], axis=1), preferred_element_type=F32)
    c_tile = c_parts[:, :F_PAD] + c_parts[:, F_PAD:] + csum_ref[0:1, :]
    csum_ref[0:1, :] = c_tile[T - 1:T, :]

    lane = lax.broadcasted_iota(jnp.int32, (T, PAIR_W), 1)
    low = lane < HEAD_DIM
    pq = proj(2 * W, W) * (QK_SCALE * LOG2E)
    pk = proj(3 * W, W)
    pv = proj(4 * W, W)
    for hh in range(N_HEADS):
        blk = slice((hh // 2) * PAIR_W, (hh // 2 + 1) * PAIR_W)
        own = low if hh % 2 == 0 else jnp.logical_not(low)
        a0 = HEAD_DIM if hh % 2 == 0 else 0
        x = jnp.broadcast_to(c_tile[:, hh:hh + 1] * (-LOG2E), (T, PAIR_W))
        hi = x.astype(BF16).astype(F32)
        r1 = x - hi
        mid = r1.astype(BF16).astype(F32)
        lo = r1 - mid
        c_parts = jnp.where(lane == a0, hi, jnp.where(lane == a0 + 1, mid, jnp.where(lane == a0 + 2, lo, 0.0)))
        ones3 = jnp.where((lane >= a0) & (lane < a0 + 3), 1.0, 0.0)
        one1 = jnp.where(lane == a0, 1.0, 0.0)
        q_ref[0, hh] = jnp.where(own, pq[:, blk], ones3).astype(BF16)
        k_ref[0, hh] = jnp.where(own, pk[:, blk], c_parts).astype(BF16)
        v_ref[0, hh] = jnp.where(own, pv[:, blk], one1).astype(BF16)

    def max_block_norm(x):
        sq = [jnp.max(jnp.sum(x[:, p * PAIR_W:(p + 1) * PAIR_W] ** 2, axis=-1, keepdims=True), axis=0, keepdims=True)
              for p in range(N_PAIRS)]
        return jnp.broadcast_to(jnp.sqrt(functools.reduce(jnp.maximum, sq)), (1, PAIR_W))

    st_ref[0, 0] = jnp.concatenate([c_tile[0:1, :], c_tile[T - 1:T, :], max_block_norm(pq), max_block_norm(pk),
                                    c_tile[T // 2 - 1:T // 2, :], jnp.zeros((3, PAIR_W), F32)], axis=0)

    u_ref[...] = _gelu(proj(0, W))
    g = _gelu(proj(W, W))
    ms = jnp.dot((g * g).astype(BF16), seg_ref[...], preferred_element_type=F32) * (1.0 / HEAD_DIM)
    gv_ref[...] = (g * lax.rsqrt(ms + EPS) * gn_ref[...]).astype(BF16)

    HW = HEADS_PER_HALF * HEAD_DIM
    lane_head = lax.broadcasted_iota(jnp.int32, (SGU_CHUNK, HW), 1) // HEAD_DIM
    wrow = lax.broadcasted_iota(jnp.int32, (SGU_CHUNK, HEADS_PER_HALF * SGU_CHUNK), 0)
    wcol = lax.broadcasted_iota(jnp.int32, (SGU_CHUNK, HEADS_PER_HALF * SGU_CHUNK), 1) % SGU_CHUNK
    for half in range(2):
        cols = slice(half * HW, (half + 1) * HW)
        w_causal = jnp.where(wcol <= wrow, sw_ref[half], 0.0).astype(BF16)
        bias = sb_ref[:, cols]
        for c in range(T // SGU_CHUNK):
            rows = slice(c * SGU_CHUNK, (c + 1) * SGU_CHUNK)
            gvc = gv_ref[rows, cols]
            stacked = jnp.concatenate(
                [jnp.where(lane_head == j, gvc, jnp.zeros_like(gvc)) for j in range(HEADS_PER_HALF)], axis=0)
            mixed = jnp.dot(w_causal, stacked, preferred_element_type=F32) + bias
            yc_ref[0, rows, cols] = (u_ref[rows, cols] * mixed).astype(BF16)


def _odd_in(h, layer, g, w_in, sgu_norm, sgu_w, sgu_b, fox_bf):
    B, S, D = h.shape
    T, W = OD_T, MIX_W
    n_in = 5 * W + F_PAD
    win = jnp.pad(w_in, ((0, 0), (0, 0), (0, n_in - w_in.shape[2]))).astype(BF16)
    seg = _block_diag(jnp.ones((N_HEADS, HEAD_DIM, HEAD_DIM), F32)).astype(BF16)
    sw = sgu_w[layer].reshape(2, HEADS_PER_HALF, SGU_CHUNK, SGU_CHUNK).transpose(0, 2, 1, 3).reshape(
        2, SGU_CHUNK, HEADS_PER_HALF * SGU_CHUNK)
    sb = jnp.repeat(sgu_b[layer].T, HEAD_DIM, axis=1)
    bf = jnp.pad(fox_bf[layer], (0, F_PAD - N_HEADS)).reshape(1, F_PAD)
    tile = pl.BlockSpec((1, T, D), lambda b, t: (b, t, 0))
    half_tile = pl.BlockSpec((1, T, W), lambda b, t: (b, t, 0))
    head_tile = pl.BlockSpec((1, N_HEADS, T, PAIR_W), lambda b, t: (b, 0, t, 0))
    act = jax.ShapeDtypeStruct((B, S, W), BF16)
    per_head = jax.ShapeDtypeStruct((B, N_HEADS, S, PAIR_W), BF16)
    return pl.pallas_call(
        _odd_in_kernel,
        out_shape=(act, per_head, per_head, per_head, jax.ShapeDtypeStruct((B, S // T, 8, PAIR_W), F32)),
        grid=(B, S // T),
        in_specs=[tile, _layer_block((1, D), layer), _layer_block((D, n_in), layer), _layer_block((1, W), layer),
                  _resident((W, W)), _resident((2, SGU_CHUNK, HEADS_PER_HALF * SGU_CHUNK)),
                  _resident((SGU_CHUNK, W)), _resident((1, F_PAD))],
        out_specs=(half_tile, head_tile, head_tile, head_tile,
                   pl.BlockSpec((1, 1, 8, PAIR_W), lambda b, t: (b, t, 0, 0))),
        scratch_shapes=[pltpu.VMEM((T, D), BF16),
                        pltpu.VMEM((T, W), BF16),
                        pltpu.VMEM((T, W), F32),
                        pltpu.VMEM((N_HEADS, 128), F32)],
        compiler_params=pltpu.CompilerParams(dimension_semantics=("arbitrary", "arbitrary"),
                                             vmem_limit_bytes=VMEM_LIMIT_BYTES),
        name="odd_in",
    )(h, _rows(g), win, _rows(sgu_norm), seg, sw, sb, bf)


AT_T = 512
AT_K = 256
KEY_BLOCKS_PER_TILE = AT_T // AT_K
M_INIT = -1e30
QK_AHEAD = 2
SKIP_LOG2 = -160.0
NORM_SLACK = 1.05


def _first_needed_block(stats):
    KB = KEY_BLOCKS_PER_TILE
    batch, n = stats.shape[:2]
    c_first = stats[:, :, 0, :N_HEADS]
    c_end = jnp.stack([stats[:, :, 4, :N_HEADS], stats[:, :, 1, :N_HEADS]], axis=2).reshape(batch, KB * n, N_HEADS)
    q_norm = stats[:, :, 2, 0]
    k_norm = jnp.max(stats[:, :, 3, 0], axis=1, keepdims=True)
    spread = NORM_SLACK * 2.0 * q_norm * k_norm
    decay = (c_end[:, None, :, :] - c_first[:, :, None, :]) * LOG2E
    block = lax.broadcasted_iota(jnp.int32, (n, KB * n), 1)
    tile = lax.broadcasted_iota(jnp.int32, (n, KB * n), 0)
    worst = jnp.max(spread[:, :, None, None] - decay, axis=-1)
    needed = (worst >= SKIP_LOG2) | (block >= KB * tile)[None]
    return jnp.min(jnp.where(needed, block[None], KB * n), axis=-1).astype(jnp.int32)


def _attn_kernel(first_ref, q_ref, k_ref, v_ref, yc_ref, h_ref, wout_ref, o_ref,
                 s_ref, mx_ref, m_ref, acc_ref, yd_ref):
    T, K, W, KB = AT_T, AT_K, MIX_W, KEY_BLOCKS_PER_TILE
    qi = pl.program_id(1)

    def run_blocks(blocks, start=False):
        assert not start or blocks[0][1] == 0
        items = [(blk, hh) for blk in blocks for hh in range(N_HEADS)]

        def logits(item):
            (kb, r0, mask), hh = item
            rows = slice(r0, T)
            k0 = pl.multiple_of(kb * K, K)
            s = lax.dot_general(q_ref[0, hh, rows, :], k_ref[0, hh, pl.ds(k0, K), :], (((1,), (1,)), ((), ())),
                                preferred_element_type=F32)
            if mask == 'diagonal':
                visible = (lax.broadcasted_iota(jnp.int32, (T - r0, K), 1)
                           <= lax.broadcasted_iota(jnp.int32, (T - r0, K), 0))
                s = jnp.where(visible, s, -jnp.inf)
            elif mask == 'if_not_first_tile':
                s = jnp.where(qi > 0, s, -jnp.inf)
            s_ref[hh, rows, :] = s
            mx_ref[hh, rows, :] = jnp.broadcast_to(jnp.max(s, axis=-1, keepdims=True), (T - r0, PAIR_W))

        def update(item, init):
            (kb, r0, _), hh = item
            rows = slice(r0, T)
            k0 = pl.multiple_of(kb * K, K)
            if init:
                m_new = jnp.maximum(mx_ref[hh, rows, :], M_INIT)
            else:
                m_old = m_ref[hh, rows, :]
                m_new = jnp.maximum(m_old, mx_ref[hh, rows, :])
            p = jnp.exp2(s_ref[hh, rows, :] - jnp.concatenate([m_new] * (K // PAIR_W), axis=1))
            pv = jnp.dot(p.astype(BF16), v_ref[0, hh, pl.ds(k0, K), :], preferred_element_type=F32)
            m_ref[hh, rows, :] = m_new
            acc_ref[hh, rows, :] = pv if init else jnp.exp2(m_old - m_new) * acc_ref[hh, rows, :] + pv

        for item in items[:QK_AHEAD]:
            logits(item)
        for i, item in enumerate(items):
            if i + QK_AHEAD < len(items):
                logits(items[i + QK_AHEAD])
            update(item, init=start and i < N_HEADS)

    def far_block(kb, carry):
        run_blocks([(kb, 0, None)])
        return carry

    prev = jnp.maximum(KB * qi - 1, 0)
    run_blocks([(prev, 0, 'if_not_first_tile')] + [(KB * qi + d, d * K, 'diagonal') for d in range(KB)], start=True)
    lax.fori_loop(first_ref[pl.program_id(0), qi], KB * qi - 1, far_block, 0)

    low = lax.broadcasted_iota(jnp.int32, (T, PAIR_W), 1) < HEAD_DIM
    for pair in range(N_PAIRS):
        acc_a = acc_ref[2 * pair]
        acc_b = acc_ref[2 * pair + 1]
        y = jnp.where(low, acc_a / acc_a[:, HEAD_DIM:HEAD_DIM + 1], acc_b / acc_b[:, 0:1])
        yd_ref[:, pair * PAIR_W:(pair + 1) * PAIR_W] = y.astype(BF16)
    o_ref[0] = (h_ref[0]
                + jnp.dot(yc_ref[0], wout_ref[0:W, :], preferred_element_type=F32)
                + jnp.dot(yd_ref[...], wout_ref[W:2 * W, :], preferred_element_type=F32))


def _attention(q, k, v, stats, yc, h, layer, w_out):
    B, S, D = h.shape
    T, K, W = AT_T, AT_K, MIX_W
    assert OD_T == AT_T and KEY_BLOCKS_PER_TILE == 2, "tile summaries hold c at the middle and end of an odd_in tile"
    q_tile = pl.BlockSpec((1, N_HEADS, T, PAIR_W), lambda b, t: (b, 0, t, 0))
    seq = pl.BlockSpec((1, N_HEADS, S, PAIR_W), lambda b, t: (b, 0, 0, 0), pipeline_mode=pl.Buffered(1))
    tile = pl.BlockSpec((1, T, D), lambda b, t: (b, t, 0))
    return pl.pallas_call(
        _attn_kernel,
        out_shape=jax.ShapeDtypeStruct((B, S, D), F32),
        grid=(B, S // T),
        in_specs=[pl.BlockSpec(memory_space=pltpu.SMEM), q_tile, seq, seq,
                  pl.BlockSpec((1, T, W), lambda b, t: (b, t, 0)), tile, _layer_block((2 * W, D), layer)],
        out_specs=tile,
        scratch_shapes=[pltpu.VMEM((N_HEADS, T, K), F32),
                        pltpu.VMEM((N_HEADS, T, PAIR_W), F32),
                        pltpu.VMEM((N_HEADS, T, PAIR_W), F32),
                        pltpu.VMEM((N_HEADS, T, PAIR_W), F32),
                        pltpu.VMEM((T, W), BF16)],
        compiler_params=pltpu.CompilerParams(dimension_semantics=("arbitrary", "arbitrary"),
                                             vmem_limit_bytes=VMEM_LIMIT_BYTES),
        name="fox_attention",
    )(_first_needed_block(stats), q, k, v, yc, h, w_out)


def kernel(x, mix0_norm, mix0_w_in, lru_conv_w, lru_conv_b, lru_wa, lru_ba, lru_wx, lru_bx, lru_lambda,
           sconv_w, sconv_b, mix0_w_out, mix1_norm, mix1_w_in, sgu_norm, sgu_w, sgu_b, fox_bf, mix1_w_out,
           ffn_norm, ffn_up, ffn_conv_w, ffn_conv_b, ffn_down, final_norm):
    ffn_up_b = ffn_up.astype(BF16)
    ffn_down_b = ffn_down.astype(BF16)
    h = _even_mixer(x, 0, mix0_norm, mix0_w_in.astype(BF16), lru_conv_w, lru_conv_b, lru_wa, lru_ba, lru_wx, lru_bx,
                    lru_lambda, sconv_w, sconv_b, mix0_w_out.astype(BF16))
    h = _ffn(h, 0, ffn_norm, ffn_up_b, ffn_conv_w, ffn_conv_b, ffn_down_b, final_norm, final_norm=False)
    yc, q, k, v, stats = _odd_in(h, 0, mix1_norm, mix1_w_in, sgu_norm, sgu_w, sgu_b, fox_bf)
    h = _attention(q, k, v, stats, yc, h, 0, mix1_w_out.astype(BF16))
    return _ffn(h, 1, ffn_norm, ffn_up_b, ffn_conv_w, ffn_conv_b, ffn_down_b, final_norm, final_norm=True)
```

```python
import functools

import jax
import jax.numpy as jnp
from jax import lax
from jax.experimental import pallas as pl
from jax.experimental.pallas import tpu as pltpu

F32 = jnp.float32
BF16 = jnp.bfloat16

EPS = 1e-6
D_MODEL = 1024
D_FF = 2816
FFN_TAPS = 3
HALO = 8
VMEM_LIMIT_BYTES = 56 * 1024 * 1024


def _rmsnorm(x, g):
    ms = jnp.mean(x * x, axis=-1, keepdims=True)
    return x * lax.rsqrt(ms + EPS) * g


def _sigmoid(x):
    return 1.0 / (1.0 + jnp.exp(-x))


def _resident(shape):
    return pl.BlockSpec(shape, lambda *_: (0,) * len(shape), pipeline_mode=pl.Buffered(1))


def _layer_block(shape, layer):
    return pl.BlockSpec((None,) + tuple(shape), lambda *_: (layer,) + (0,) * len(shape),
                        pipeline_mode=pl.Buffered(1))


def _rows(p):
    return p[:, None, :]


def _stream_to_bf16(srcs, dsts, stage_ref, sems):
    copies = [pltpu.make_async_copy(src, stage_ref.at[i % 2], sems.at[i % 2]) for i, src in enumerate(srcs)]
    copies[0].start()
    for i, dst in enumerate(dsts):
        if i + 1 < len(copies):
            copies[i + 1].start()
        copies[i].wait()
        dst[...] = stage_ref[i % 2].astype(BF16)


FFN_T = 512
FFN_CH = 256
FFN_NCH = D_FF // FFN_CH
SUBLANES = 8
FFN_PAD = (FFN_TAPS - 1) * SUBLANES


def _ffn_kernel(h_hbm, g_ref, wup_hbm, cw_ref, cb_ref, wdn_hbm, fg_ref, o_hbm,
                hbuf_ref, obuf_ref, xn_ref, work0_ref, work1_ref, act0_ref, act1_ref, halo_ref, acc_ref,
                wup_ref, wdn_ref, stage_up_ref, stage_dn_ref, sem_in, sem_out, sem_w,
                *, layer, final_norm, n_tiles, n_steps):
    T, CH, PAD = FFN_T, FFN_CH, FFN_PAD
    G = T // SUBLANES
    b = pl.program_id(0)
    t = pl.program_id(1)
    n = b * n_tiles + t
    slot = lax.rem(n, 2)

    @pl.when(n == 0)
    def _():
        UC = stage_up_ref.shape[2]
        DR = stage_dn_ref.shape[1]
        _stream_to_bf16([wup_hbm.at[layer, :, pl.ds(c * UC, UC)] for c in range(wup_ref.shape[1] // UC)],
                        [wup_ref.at[:, pl.ds(c * UC, UC)] for c in range(wup_ref.shape[1] // UC)],
                        stage_up_ref, sem_w.at[0])
        _stream_to_bf16([wdn_hbm.at[layer, pl.ds(c * DR, DR), :] for c in range(wdn_ref.shape[0] // DR)],
                        [wdn_ref.at[pl.ds(c * DR, DR), :] for c in range(wdn_ref.shape[0] // DR)],
                        stage_dn_ref, sem_w.at[1])

    def in_copies(bb, tt, sl):
        return [pltpu.make_async_copy(h_hbm.at[bb, pl.ds(tt * T + r * G, G), :], hbuf_ref.at[sl, :, r, :],
                                      sem_in.at[sl]) for r in range(SUBLANES)]

    def out_copies(bb, tt, sl):
        return [pltpu.make_async_copy(obuf_ref.at[sl, :, r, :], o_hbm.at[bb, pl.ds(tt * T + r * G, G), :],
                                      sem_out.at[sl]) for r in range(SUBLANES)]

    @pl.when(n == 0)
    def _():
        for c in in_copies(b, t, slot):
            c.start()

    @pl.when(n + 1 < n_steps)
    def _():
        n1 = n + 1
        for c in in_copies(n1 // n_tiles, lax.rem(n1, n_tiles), 1 - slot):
            c.start()

    @pl.when(t == 0)
    def _():
        halo_ref[...] = jnp.zeros(halo_ref.shape, F32)

    for c in in_copies(b, t, slot):
        c.wait()
    h = hbuf_ref[slot].reshape(T, D_MODEL)
    xn_ref[...] = _rmsnorm(h, g_ref[...]).astype(BF16)
    acc_ref[...] = h

    works = (work0_ref, work1_ref)
    acts = (act0_ref, act1_ref)
    first_row = lax.broadcasted_iota(jnp.int32, (SUBLANES, CH), 0) == 0

    def chunk_cols(j):
        return slice(j * CH, (j + 1) * CH), slice(D_FF + j * CH, D_FF + (j + 1) * CH)

    def up_proj_half(j, half):
        work_ref = works[j % 2]
        hcols = slice(half * CH, (half + 1) * CH)
        work_ref[PAD:PAD + T, hcols] = jnp.dot(xn_ref[...], wup_ref[:, chunk_cols(j)[half]],
                                               preferred_element_type=F32)
        for k in range(FFN_TAPS - 1):
            rows = slice(k * SUBLANES, (k + 1) * SUBLANES)
            tail = pltpu.roll(work_ref[T + k * SUBLANES:T + (k + 1) * SUBLANES, hcols], 1, axis=0)
            work_ref[rows, hcols] = jnp.where(first_row, halo_ref[j, rows, hcols], tail)
            halo_ref[j, rows, hcols] = tail

    def down_proj_half(j, half):
        cols = slice(half * (D_MODEL // 2), (half + 1) * (D_MODEL // 2))
        acc_ref[:, cols] += jnp.dot(acts[j % 2][...], wdn_ref[j * CH:(j + 1) * CH, cols],
                                    preferred_element_type=F32)

    up_proj_half(0, 0)
    up_proj_half(0, 1)
    for j in range(FFN_NCH):
        for half in range(2):
            if j + 1 < FFN_NCH:
                up_proj_half(j + 1, half)
            if j >= 1:
                down_proj_half(j - 1, half)
        work_ref = works[j % 2]
        gate_cols, val_cols = chunk_cols(j)
        cw = jnp.concatenate([cw_ref[:, gate_cols], cw_ref[:, val_cols]], axis=1)
        y = jnp.concatenate([cb_ref[:, gate_cols], cb_ref[:, val_cols]], axis=1)
        for k in range(FFN_TAPS):
            y = y + work_ref[k * SUBLANES:k * SUBLANES + T, :] * cw[k:k + 1, :]
        gate = y[:, :CH]
        val = y[:, CH:]
        acts[j % 2][...] = (gate * _sigmoid(gate) * val).astype(BF16)
    down_proj_half(FFN_NCH - 1, 0)
    down_proj_half(FFN_NCH - 1, 1)

    out = acc_ref[...]
    if final_norm:
        out = _rmsnorm(out, fg_ref[...])

    @pl.when(n >= 2)
    def _():
        for c in out_copies(b, t, slot):
            c.wait()

    obuf_ref[slot] = out.reshape(G, SUBLANES, D_MODEL)
    for c in out_copies(b, t, slot):
        c.start()

    @pl.when(n == n_steps - 1)
    def _():
        for c in out_copies(b, t, slot):
            c.wait()
        if n_steps >= 2:
            for c in out_copies(b, t, 1 - slot):
                c.wait()


def _ffn(h, layer, g, w_up, conv_w, conv_b, w_down, final_g, *, final_norm):
    B, S, D = h.shape
    T, CH, NCH = FFN_T, FFN_CH, FFN_NCH
    G = T // SUBLANES
    n_tiles = S // T
    return pl.pallas_call(
        functools.partial(_ffn_kernel, layer=layer, final_norm=final_norm, n_tiles=n_tiles, n_steps=B * n_tiles),
        out_shape=jax.ShapeDtypeStruct((B, S, D), F32),
        grid=(B, n_tiles),
        in_specs=[pl.BlockSpec(memory_space=pl.ANY), _layer_block((1, D), layer),
                  pl.BlockSpec(memory_space=pl.ANY), _layer_block((FFN_TAPS, 2 * D_FF), layer),
                  _layer_block((1, 2 * D_FF), layer), pl.BlockSpec(memory_space=pl.ANY), _resident((1, D))],
        out_specs=pl.BlockSpec(memory_space=pl.ANY),
        scratch_shapes=[pltpu.VMEM((2, G, SUBLANES, D), F32),
                        pltpu.VMEM((2, G, SUBLANES, D), F32),
                        pltpu.VMEM((T, D), BF16),
                        pltpu.VMEM((T + FFN_PAD, 2 * CH), F32),
                        pltpu.VMEM((T + FFN_PAD, 2 * CH), F32),
                        pltpu.VMEM((T, CH), BF16),
                        pltpu.VMEM((T, CH), BF16),
                        pltpu.VMEM((NCH, FFN_PAD, 2 * CH), F32),
                        pltpu.VMEM((T, D), F32),
                        pltpu.VMEM((D, 2 * D_FF), BF16),
                        pltpu.VMEM((D_FF, D), BF16),
                        pltpu.VMEM((2, D, 2 * CH), F32),
                        pltpu.VMEM((2, CH, D), F32),
                        pltpu.SemaphoreType.DMA((2,)),
                        pltpu.SemaphoreType.DMA((2,)),
                        pltpu.SemaphoreType.DMA((2, 2))],
        compiler_params=pltpu.CompilerParams(dimension_semantics=("arbitrary", "arbitrary"),
                                             vmem_limit_bytes=VMEM_LIMIT_BYTES),
        name="ffn_final" if final_norm else "ffn",
    )(h, _rows(g), w_up, conv_w, _rows(conv_b), w_down, final_g.reshape(1, D))


EV_T = 512
MIX_W = 512
LRU_TAPS = 4
SC_TAPS = 3
LRU_DECAY_C = 8.0
GELU_C = 0.7978845608028654


def _gelu(x):
    return x * (0.5 * (1.0 + jnp.tanh(GELU_C * (x + 0.044715 * (x * x * x)))))


def _causal_taps(buf_ref, w_ref, b_ref, taps, T):
    w = w_ref[...]
    y = b_ref[...]
    for k in range(taps):
        off = HALO - (taps - 1) + k
        y = y + buf_ref[off:off + T, :] * w[k:k + 1, :]
    return y


def _even_kernel(h_ref, g_ref, win_ref, lcw_ref, lcb_ref, wg_ref, bg_ref, lam_ref, scw_ref, scb_ref, wout_ref,
                 o_ref, xn_ref, xa_ref, cv_ref, a_ref, hs_ref, y_ref, state_ref):
    T, W = EV_T, MIX_W

    @pl.when(pl.program_id(1) == 0)
    def _():
        xa_ref[0:HALO, :] = jnp.zeros((HALO, W), F32)
        cv_ref[0:HALO, :] = jnp.zeros((HALO, W), F32)
        state_ref[...] = jnp.zeros(state_ref.shape, F32)

    h = h_ref[0]
    xn_ref[...] = _rmsnorm(h, g_ref[...]).astype(BF16)

    def proj(c):
        return jnp.dot(xn_ref[...], win_ref[:, c * W:(c + 1) * W], preferred_element_type=F32)

    xa_ref[HALO:HALO + T, :] = proj(0)
    xc = _causal_taps(xa_ref, lcw_ref, lcb_ref, LRU_TAPS, T)
    xa_ref[0:HALO, :] = xa_ref[T:T + HALO, :]
    xcb = xc.astype(BF16)
    HW = W // 2
    pre = [jnp.dot(xcb[:, s * HW:(s + 1) * HW], wg_ref[s], preferred_element_type=F32) for s in range(2)]
    bg = bg_ref[...]
    r = _sigmoid(jnp.concatenate([pre[0][:, :HW], pre[1][:, :HW]], axis=1) + bg[:, :W])
    i = _sigmoid(jnp.concatenate([pre[0][:, HW:], pre[1][:, HW:]], axis=1) + bg[:, W:])
    lam = lam_ref[...]
    softplus_neg_lam = jnp.maximum(-lam, 0.0) + jnp.log1p(jnp.exp(-jnp.abs(lam)))
    log_a = (-LRU_DECAY_C * r) * softplus_neg_lam
    a = jnp.exp(log_a)
    th = jnp.tanh(log_a)
    u = jnp.sqrt((-2.0 * th) / (1.0 - th)) * (i * xc)

    G = T // 8
    a3 = a.reshape(G, 8, W)
    u3 = u.reshape(G, 8, W)
    row = lax.broadcasted_iota(jnp.int32, (G, 8, W), 1)
    for d in (1, 2, 4):
        keep = row >= d
        u3 = u3 + a3 * jnp.where(keep, pltpu.roll(u3, d, axis=1), 0.0)
        a3 = a3 * jnp.where(keep, pltpu.roll(a3, d, axis=1), 1.0)
    a_ref[...] = a3.reshape(T, W)
    hs_ref[...] = u3.reshape(T, W)
    carry = state_ref[0:1, :]
    for gi in range(G):
        rows = slice(gi * 8, gi * 8 + 8)
        hg = hs_ref[rows, :] + a_ref[rows, :] * carry
        hs_ref[rows, :] = hg
        carry = hg[7:8, :]
    state_ref[0:1, :] = carry
    y_ref[:, 0:W] = (hs_ref[...] * _gelu(proj(1))).astype(BF16)

    cv_ref[HALO:HALO + T, :] = proj(2) * proj(4)
    cc = _causal_taps(cv_ref, scw_ref, scb_ref, SC_TAPS, T)
    cv_ref[0:HALO, :] = cv_ref[T:T + HALO, :]
    y_ref[:, W:2 * W] = (proj(3) * cc).astype(BF16)

    o_ref[0] = h + jnp.dot(y_ref[...], wout_ref[...], preferred_element_type=F32)


def _block_diag(w):
    H, d, _ = w.shape
    return jnp.einsum('hij,hg->higj', w, jnp.eye(H, dtype=w.dtype)).reshape(H * d, H * d)


def _even_mixer(h, layer, g, w_in, lcw, lcb, wa, ba, wx, bx, lam, scw, scb, w_out):
    B, S, D = h.shape
    T, W = EV_T, MIX_W
    nh = wa.shape[1] // 2
    wg = jnp.stack([jnp.concatenate([_block_diag(wa[layer, s * nh:(s + 1) * nh]),
                                     _block_diag(wx[layer, s * nh:(s + 1) * nh])], axis=1) for s in range(2)])
    wg = wg.astype(BF16)
    bg = jnp.concatenate([ba[layer], bx[layer]]).reshape(1, 2 * W)
    tile = pl.BlockSpec((1, T, D), lambda b, t: (b, t, 0))
    return pl.pallas_call(
        _even_kernel,
        out_shape=jax.ShapeDtypeStruct((B, S, D), F32),
        grid=(B, S // T),
        in_specs=[tile, _layer_block((1, D), layer), _layer_block((D, 5 * W), layer),
                  _layer_block((LRU_TAPS, W), layer), _layer_block((1, W), layer),
                  _resident((2, W // 2, W)), _resident((1, 2 * W)), _layer_block((1, W), layer),
                  _layer_block((SC_TAPS, W), layer), _layer_block((1, W), layer), _layer_block((2 * W, D), layer)],
        out_specs=tile,
        scratch_shapes=[pltpu.VMEM((T, D), BF16),
                        pltpu.VMEM((T + HALO, W), F32),
                        pltpu.VMEM((T + HALO, W), F32),
                        pltpu.VMEM((T, W), F32),
                        pltpu.VMEM((T, W), F32),
                        pltpu.VMEM((T, 2 * W), BF16),
                        pltpu.VMEM((8, W), F32)],
        compiler_params=pltpu.CompilerParams(dimension_semantics=("arbitrary", "arbitrary"),
                                             vmem_limit_bytes=VMEM_LIMIT_BYTES),
        name="even_mixer",
    )(h, _rows(g), w_in, lcw, _rows(lcb), wg, bg, _rows(lam), scw, _rows(scb), w_out)


OD_T = 512
SGU_CHUNK = 128
N_HEADS = 8
HEAD_DIM = 64
F_PAD = 128
HEADS_PER_HALF = 4
QK_SCALE = HEAD_DIM ** -0.5
LOG2E = 1.4426950408889634
PAIR_W = 2 * HEAD_DIM
N_PAIRS = N_HEADS // 2


def _odd_in_kernel(h_ref, g_ref, win_ref, gn_ref, seg_ref, sw_ref, sb_ref, bf_ref,
                   yc_ref, q_ref, k_ref, v_ref, st_ref, xn_ref, gv_ref, u_ref, csum_ref):
    T, W = OD_T, MIX_W

    @pl.when(pl.program_id(1) == 0)
    def _():
        csum_ref[...] = jnp.zeros(csum_ref.shape, F32)

    xn_ref[...] = _rmsnorm(h_ref[0], g_ref[...]).astype(BF16)

    def proj(lo, width):
        return jnp.dot(xn_ref[...], win_ref[:, lo:lo + width], preferred_element_type=F32)

    fl = proj(5 * W, F_PAD) + bf_ref[...]
    logf = jnp.minimum(fl, 0.0) - jnp.log1p(jnp.exp(-jnp.abs(fl)))
    tri = (lax.broadcasted_iota(jnp.int32, (T, T), 1) <= lax.broadcasted_iota(jnp.int32, (T, T), 0)).astype(BF16)
    logf_hi = logf.astype(BF16)
    logf_mid = (logf - logf_hi.astype(F32)).astype(BF16)
    c_parts = jnp.dot(tri, jnp.concatenate([logf_hi, logf_mid], axis=1), preferred_element_type=F32)
    c_tile = c_parts[:, :F_PAD] + c_parts[:, F_PAD:] + csum_ref[0:1, :]
    csum_ref[0:1, :] = c_tile[T - 1:T, :]

    lane = lax.broadcasted_iota(jnp.int32, (T, PAIR_W), 1)
    low = lane < HEAD_DIM
    pq = proj(2 * W, W) * (QK_SCALE * LOG2E)
    pk = proj(3 * W, W)
    pv = proj(4 * W, W)
    for hh in range(N_HEADS):
        blk = slice((hh // 2) * PAIR_W, (hh // 2 + 1) * PAIR_W)
        own = low if hh % 2 == 0 else jnp.logical_not(low)
        a0 = HEAD_DIM if hh % 2 == 0 else 0
        x = jnp.broadcast_to(c_tile[:, hh:hh + 1] * (-LOG2E), (T, PAIR_W))
        hi = x.astype(BF16).astype(F32)
        r1 = x - hi
        mid = r1.astype(BF16).astype(F32)
        lo = r1 - mid
        c_parts = jnp.where(lane == a0, hi, jnp.where(lane == a0 + 1, mid, jnp.where(lane == a0 + 2, lo, 0.0)))
        ones3 = jnp.where((lane >= a0) & (lane < a0 + 3), 1.0, 0.0)
        one1 = jnp.where(lane == a0, 1.0, 0.0)
        q_ref[0, hh] = jnp.where(own, pq[:, blk], ones3).astype(BF16)
        k_ref[0, hh] = jnp.where(own, pk[:, blk], c_parts).astype(BF16)
        v_ref[0, hh] = jnp.where(own, pv[:, blk], one1).astype(BF16)

    def max_block_norm(x):
        sq = [jnp.max(jnp.sum(x[:, p * PAIR_W:(p + 1) * PAIR_W] ** 2, axis=-1, keepdims=True), axis=0, keepdims=True)
              for p in range(N_PAIRS)]
        return jnp.broadcast_to(jnp.sqrt(functools.reduce(jnp.maximum, sq)), (1, PAIR_W))

    st_ref[0, 0] = jnp.concatenate([c_tile[0:1, :], c_tile[T - 1:T, :], max_block_norm(pq), max_block_norm(pk),
                                    c_tile[T // 2 - 1:T // 2, :], jnp.zeros((3, PAIR_W), F32)], axis=0)

    u_ref[...] = _gelu(proj(0, W))
    g = _gelu(proj(W, W))
    ms = jnp.dot((g * g).astype(BF16), seg_ref[...], preferred_element_type=F32) * (1.0 / HEAD_DIM)
    gv_ref[...] = (g * lax.rsqrt(ms + EPS) * gn_ref[...]).astype(BF16)

    HW = HEADS_PER_HALF * HEAD_DIM
    lane_head = lax.broadcasted_iota(jnp.int32, (SGU_CHUNK, HW), 1) // HEAD_DIM
    wrow = lax.broadcasted_iota(jnp.int32, (SGU_CHUNK, HEADS_PER_HALF * SGU_CHUNK), 0)
    wcol = lax.broadcasted_iota(jnp.int32, (SGU_CHUNK, HEADS_PER_HALF * SGU_CHUNK), 1) % SGU_CHUNK
    for half in range(2):
        cols = slice(half * HW, (half + 1) * HW)
        w_causal = jnp.where(wcol <= wrow, sw_ref[half], 0.0).astype(BF16)
        bias = sb_ref[:, cols]
        for c in range(T // SGU_CHUNK):
            rows = slice(c * SGU_CHUNK, (c + 1) * SGU_CHUNK)
            gvc = gv_ref[rows, cols]
            stacked = jnp.concatenate(
                [jnp.where(lane_head == j, gvc, jnp.zeros_like(gvc)) for j in range(HEADS_PER_HALF)], axis=0)
            mixed = jnp.dot(w_causal, stacked, preferred_element_type=F32) + bias
            yc_ref[0, rows, cols] = (u_ref[rows, cols] * mixed).astype(BF16)


def _odd_in(h, layer, g, w_in, sgu_norm, sgu_w, sgu_b, fox_bf):
    B, S, D = h.shape
    T, W = OD_T, MIX_W
    n_in = 5 * W + F_PAD
    win = jnp.pad(w_in, ((0, 0), (0, 0), (0, n_in - w_in.shape[2]))).astype(BF16)
    seg = _block_diag(jnp.ones((N_HEADS, HEAD_DIM, HEAD_DIM), F32)).astype(BF16)
    sw = sgu_w[layer].reshape(2, HEADS_PER_HALF, SGU_CHUNK, SGU_CHUNK).transpose(0, 2, 1, 3).reshape(
        2, SGU_CHUNK, HEADS_PER_HALF * SGU_CHUNK)
    sb = jnp.repeat(sgu_b[layer].T, HEAD_DIM, axis=1)
    bf = jnp.pad(fox_bf[layer], (0, F_PAD - N_HEADS)).reshape(1, F_PAD)
    tile = pl.BlockSpec((1, T, D), lambda b, t: (b, t, 0))
    half_tile = pl.BlockSpec((1, T, W), lambda b, t: (b, t, 0))
    head_tile = pl.BlockSpec((1, N_HEADS, T, PAIR_W), lambda b, t: (b, 0, t, 0))
    act = jax.ShapeDtypeStruct((B, S, W), BF16)
    per_head = jax.ShapeDtypeStruct((B, N_HEADS, S, PAIR_W), BF16)
    return pl.pallas_call(
        _odd_in_kernel,
        out_shape=(act, per_head, per_head, per_head, jax.ShapeDtypeStruct((B, S // T, 8, PAIR_W), F32)),
        grid=(B, S // T),
        in_specs=[tile, _layer_block((1, D), layer), _layer_block((D, n_in), layer), _layer_block((1, W), layer),
                  _resident((W, W)), _resident((2, SGU_CHUNK, HEADS_PER_HALF * SGU_CHUNK)),
                  _resident((SGU_CHUNK, W)), _resident((1, F_PAD))],
        out_specs=(half_tile, head_tile, head_tile, head_tile,
                   pl.BlockSpec((1, 1, 8, PAIR_W), lambda b, t: (b, t, 0, 0))),
        scratch_shapes=[pltpu.VMEM((T, D), BF16),
                        pltpu.VMEM((T, W), BF16),
                        pltpu.VMEM((T, W), F32),
                        pltpu.VMEM((N_HEADS, 128), F32)],
        compiler_params=pltpu.CompilerParams(dimension_semantics=("arbitrary", "arbitrary"),
                                             vmem_limit_bytes=VMEM_LIMIT_BYTES),
        name="odd_in",
    )(h, _rows(g), win, _rows(sgu_norm), seg, sw, sb, bf)


AT_T = 512
AT_K = 256
KEY_BLOCKS_PER_TILE = AT_T // AT_K
M_INIT = -1e30
QK_AHEAD = 2
SKIP_LOG2 = -160.0
NORM_SLACK = 1.05


def _first_needed_block(stats):
    KB = KEY_BLOCKS_PER_TILE
    batch, n = stats.shape[:2]
    c_first = stats[:, :, 0, :N_HEADS]
    c_end = jnp.stack([stats[:, :, 4, :N_HEADS], stats[:, :, 1, :N_HEADS]], axis=2).reshape(batch, KB * n, N_HEADS)
    q_norm = stats[:, :, 2, 0]
    k_norm = jnp.max(stats[:, :, 3, 0], axis=1, keepdims=True)
    spread = NORM_SLACK * 2.0 * q_norm * k_norm
    decay = (c_end[:, None, :, :] - c_first[:, :, None, :]) * LOG2E
    block = lax.broadcasted_iota(jnp.int32, (n, KB * n), 1)
    tile = lax.broadcasted_iota(jnp.int32, (n, KB * n), 0)
    worst = jnp.max(spread[:, :, None, None] - decay, axis=-1)
    needed = (worst >= SKIP_LOG2) | (block >= KB * tile)[None]
    return jnp.min(jnp.where(needed, block[None], KB * n), axis=-1).astype(jnp.int32)


def _attn_kernel(first_ref, q_ref, k_ref, v_ref, yc_ref, h_ref, wout_ref, o_ref,
                 s_ref, mx_ref, m_ref, acc_ref, yd_ref):
    T, K, W, KB = AT_T, AT_K, MIX_W, KEY_BLOCKS_PER_TILE
    qi = pl.program_id(1)

    def run_blocks(blocks, start=False):
        assert not start or blocks[0][1] == 0
        items = [(blk, hh) for blk in blocks for hh in range(N_HEADS)]

        def logits(item):
            (kb, r0, mask), hh = item
            rows = slice(r0, T)
            k0 = pl.multiple_of(kb * K, K)
            s = lax.dot_general(q_ref[0, hh, rows, :], k_ref[0, hh, pl.ds(k0, K), :], (((1,), (1,)), ((), ())),
                                preferred_element_type=F32)
            if mask == 'diagonal':
                visible = (lax.broadcasted_iota(jnp.int32, (T - r0, K), 1)
                           <= lax.broadcasted_iota(jnp.int32, (T - r0, K), 0))
                s = jnp.where(visible, s, -jnp.inf)
            elif mask == 'if_not_first_tile':
                s = jnp.where(qi > 0, s, -jnp.inf)
            s_ref[hh, rows, :] = s
            mx_ref[hh, rows, :] = jnp.broadcast_to(jnp.max(s, axis=-1, keepdims=True), (T - r0, PAIR_W))

        def update(item, init):
            (kb, r0, _), hh = item
            rows = slice(r0, T)
            k0 = pl.multiple_of(kb * K, K)
            if init:
                m_new = jnp.maximum(mx_ref[hh, rows, :], M_INIT)
            else:
                m_old = m_ref[hh, rows, :]
                m_new = jnp.maximum(m_old, mx_ref[hh, rows, :])
            p = jnp.exp2(s_ref[hh, rows, :] - jnp.concatenate([m_new] * (K // PAIR_W), axis=1))
            pv = jnp.dot(p.astype(BF16), v_ref[0, hh, pl.ds(k0, K), :], preferred_element_type=F32)
            m_ref[hh, rows, :] = m_new
            acc_ref[hh, rows, :] = pv if init else jnp.exp2(m_old - m_new) * acc_ref[hh, rows, :] + pv

        for item in items[:QK_AHEAD]:
            logits(item)
        for i, item in enumerate(items):
            if i + QK_AHEAD < len(items):
                logits(items[i + QK_AHEAD])
            update(item, init=start and i < N_HEADS)

    def far_block(kb, carry):
        run_blocks([(kb, 0, None)])
        return carry

    prev = jnp.maximum(KB * qi - 1, 0)
    run_blocks([(prev, 0, 'if_not_first_tile')] + [(KB * qi + d, d * K, 'diagonal') for d in range(KB)], start=True)
    lax.fori_loop(first_ref[pl.program_id(0), qi], KB * qi - 1, far_block, 0)

    low = lax.broadcasted_iota(jnp.int32, (T, PAIR_W), 1) < HEAD_DIM
    for pair in range(N_PAIRS):
        acc_a = acc_ref[2 * pair]
        acc_b = acc_ref[2 * pair + 1]
        y = jnp.where(low, acc_a / acc_a[:, HEAD_DIM:HEAD_DIM + 1], acc_b / acc_b[:, 0:1])
        yd_ref[:, pair * PAIR_W:(pair + 1) * PAIR_W] = y.astype(BF16)
    o_ref[0] = (h_ref[0]
                + jnp.dot(yc_ref[0], wout_ref[0:W, :], preferred_element_type=F32)
                + jnp.dot(yd_ref[...], wout_ref[W:2 * W, :], preferred_element_type=F32))


def _attention(q, k, v, stats, yc, h, layer, w_out):
    B, S, D = h.shape
    T, K, W = AT_T, AT_K, MIX_W
    assert OD_T == AT_T and KEY_BLOCKS_PER_TILE == 2, "tile summaries hold c at the middle and end of an odd_in tile"
    q_tile = pl.BlockSpec((1, N_HEADS, T, PAIR_W), lambda b, t: (b, 0, t, 0))
    seq = pl.BlockSpec((1, N_HEADS, S, PAIR_W), lambda b, t: (b, 0, 0, 0), pipeline_mode=pl.Buffered(1))
    tile = pl.BlockSpec((1, T, D), lambda b, t: (b, t, 0))
    return pl.pallas_call(
        _attn_kernel,
        out_shape=jax.ShapeDtypeStruct((B, S, D), F32),
        grid=(B, S // T),
        in_specs=[pl.BlockSpec(memory_space=pltpu.SMEM), q_tile, seq, seq,
                  pl.BlockSpec((1, T, W), lambda b, t: (b, t, 0)), tile, _layer_block((2 * W, D), layer)],
        out_specs=tile,
        scratch_shapes=[pltpu.VMEM((N_HEADS, T, K), F32),
                        pltpu.VMEM((N_HEADS, T, PAIR_W), F32),
                        pltpu.VMEM((N_HEADS, T, PAIR_W), F32),
                        pltpu.VMEM((N_HEADS, T, PAIR_W), F32),
                        pltpu.VMEM((T, W), BF16)],
        compiler_params=pltpu.CompilerParams(dimension_semantics=("arbitrary", "arbitrary"),
                                             vmem_limit_bytes=VMEM_LIMIT_BYTES),
        name="fox_attention",
    )(_first_needed_block(stats), q, k, v, yc, h, w_out)


def kernel(x, mix0_norm, mix0_w_in, lru_conv_w, lru_conv_b, lru_wa, lru_ba, lru_wx, lru_bx, lru_lambda,
           sconv_w, sconv_b, mix0_w_out, mix1_norm, mix1_w_in, sgu_norm, sgu_w, sgu_b, fox_bf, mix1_w_out,
           ffn_norm, ffn_up, ffn_conv_w, ffn_conv_b, ffn_down, final_norm):
    h = _even_mixer(x, 0, mix0_norm, mix0_w_in.astype(BF16), lru_conv_w, lru_conv_b, lru_wa, lru_ba, lru_wx, lru_bx,
                    lru_lambda, sconv_w, sconv_b, mix0_w_out.astype(BF16))
    h = _ffn(h, 0, ffn_norm, ffn_up, ffn_conv_w, ffn_conv_b, ffn_down, final_norm, final_norm=False)
    yc, q, k, v, stats = _odd_in(h, 0, mix1_norm, mix1_w_in, sgu_norm, sgu_w, sgu_b, fox_bf)
    h = _attention(q, k, v, stats, yc, h, 0, mix1_w_out.astype(BF16))
    return _ffn(h, 1, ffn_norm, ffn_up, ffn_conv_w, ffn_conv_b, ffn_down, final_norm, final_norm=True)
```

```python
import functools

import jax
import jax.numpy as jnp
from jax import lax
from jax.experimental import pallas as pl
from jax.experimental.pallas import tpu as pltpu

F32 = jnp.float32
BF16 = jnp.bfloat16

EPS = 1e-6
D_MODEL = 1024
D_FF = 2816
FFN_TAPS = 3
HALO = 8
VMEM_LIMIT_BYTES = 56 * 1024 * 1024


def _rmsnorm(x, g):
    ms = jnp.mean(x * x, axis=-1, keepdims=True)
    return x * lax.rsqrt(ms + EPS) * g


def _sigmoid(x):
    return 1.0 / (1.0 + jnp.exp(-x))


def _resident(shape):
    return pl.BlockSpec(shape, lambda *_: (0,) * len(shape), pipeline_mode=pl.Buffered(1))


def _layer_block(shape, layer):
    return pl.BlockSpec((None,) + tuple(shape), lambda *_: (layer,) + (0,) * len(shape),
                        pipeline_mode=pl.Buffered(1))


def _rows(p):
    return p[:, None, :]


def _stream_to_bf16(srcs, dsts, stage_ref, sems):
    copies = [pltpu.make_async_copy(src, stage_ref.at[i % 2], sems.at[i % 2]) for i, src in enumerate(srcs)]
    copies[0].start()
    for i, dst in enumerate(dsts):
        if i + 1 < len(copies):
            copies[i + 1].start()
        copies[i].wait()
        dst[...] = stage_ref[i % 2].astype(BF16)


FFN_T = 512
FFN_CH = 256
FFN_NCH = D_FF // FFN_CH
SUBLANES = 8
FFN_PAD = (FFN_TAPS - 1) * SUBLANES


def _ffn_kernel(h_hbm, g_ref, wup_hbm, cw_ref, cb_ref, wdn_hbm, fg_ref, o_hbm,
                hbuf_ref, obuf_ref, xn_ref, work0_ref, work1_ref, act0_ref, act1_ref, halo_ref, acc_ref,
                wup_ref, wdn_ref, stage_up_ref, stage_dn_ref, sem_in, sem_out, sem_w,
                *, layer, final_norm, n_tiles, n_steps):
    T, CH, PAD = FFN_T, FFN_CH, FFN_PAD
    G = T // SUBLANES
    b = pl.program_id(0)
    t = pl.program_id(1)
    n = b * n_tiles + t
    slot = lax.rem(n, 2)

    @pl.when(n == 0)
    def _():
        UC = stage_up_ref.shape[2]
        DR = stage_dn_ref.shape[1]
        _stream_to_bf16([wup_hbm.at[layer, :, pl.ds(c * UC, UC)] for c in range(wup_ref.shape[1] // UC)],
                        [wup_ref.at[:, pl.ds(c * UC, UC)] for c in range(wup_ref.shape[1] // UC)],
                        stage_up_ref, sem_w.at[0])
        _stream_to_bf16([wdn_hbm.at[layer, pl.ds(c * DR, DR), :] for c in range(wdn_ref.shape[0] // DR)],
                        [wdn_ref.at[pl.ds(c * DR, DR), :] for c in range(wdn_ref.shape[0] // DR)],
                        stage_dn_ref, sem_w.at[1])

    def in_copies(bb, tt, sl):
        return [pltpu.make_async_copy(h_hbm.at[bb, pl.ds(tt * T + r * G, G), :], hbuf_ref.at[sl, :, r, :],
                                      sem_in.at[sl]) for r in range(SUBLANES)]

    def out_copies(bb, tt, sl):
        return [pltpu.make_async_copy(obuf_ref.at[sl, :, r, :], o_hbm.at[bb, pl.ds(tt * T + r * G, G), :],
                                      sem_out.at[sl]) for r in range(SUBLANES)]

    @pl.when(n == 0)
    def _():
        for c in in_copies(b, t, slot):
            c.start()

    @pl.when(n + 1 < n_steps)
    def _():
        n1 = n + 1
        for c in in_copies(n1 // n_tiles, lax.rem(n1, n_tiles), 1 - slot):
            c.start()

    @pl.when(t == 0)
    def _():
        halo_ref[...] = jnp.zeros(halo_ref.shape, F32)

    for c in in_copies(b, t, slot):
        c.wait()
    h = hbuf_ref[slot].reshape(T, D_MODEL)
    xn_ref[...] = _rmsnorm(h, g_ref[...]).astype(BF16)
    acc_ref[...] = h

    works = (work0_ref, work1_ref)
    acts = (act0_ref, act1_ref)
    first_row = lax.broadcasted_iota(jnp.int32, (SUBLANES, CH), 0) == 0

    def chunk_cols(j):
        return slice(j * CH, (j + 1) * CH), slice(D_FF + j * CH, D_FF + (j + 1) * CH)

    def up_proj_half(j, half):
        work_ref = works[j % 2]
        hcols = slice(half * CH, (half + 1) * CH)
        work_ref[PAD:PAD + T, hcols] = jnp.dot(xn_ref[...], wup_ref[:, chunk_cols(j)[half]],
                                               preferred_element_type=F32)
        for k in range(FFN_TAPS - 1):
            rows = slice(k * SUBLANES, (k + 1) * SUBLANES)
            tail = pltpu.roll(work_ref[T + k * SUBLANES:T + (k + 1) * SUBLANES, hcols], 1, axis=0)
            work_ref[rows, hcols] = jnp.where(first_row, halo_ref[j, rows, hcols], tail)
            halo_ref[j, rows, hcols] = tail

    def down_proj_half(j, half):
        cols = slice(half * (D_MODEL // 2), (half + 1) * (D_MODEL // 2))
        acc_ref[:, cols] += jnp.dot(acts[j % 2][...], wdn_ref[j * CH:(j + 1) * CH, cols],
                                    preferred_element_type=F32)

    up_proj_half(0, 0)
    up_proj_half(0, 1)
    for j in range(FFN_NCH):
        for half in range(2):
            if j + 1 < FFN_NCH:
                up_proj_half(j + 1, half)
            if j >= 1:
                down_proj_half(j - 1, half)
        work_ref = works[j % 2]
        gate_cols, val_cols = chunk_cols(j)
        cw = jnp.concatenate([cw_ref[:, gate_cols], cw_ref[:, val_cols]], axis=1)
        y = jnp.concatenate([cb_ref[:, gate_cols], cb_ref[:, val_cols]], axis=1)
        for k in range(FFN_TAPS):
            y = y + work_ref[k * SUBLANES:k * SUBLANES + T, :] * cw[k:k + 1, :]
        gate = y[:, :CH]
        val = y[:, CH:]
        acts[j % 2][...] = (gate * _sigmoid(gate) * val).astype(BF16)
    down_proj_half(FFN_NCH - 1, 0)
    down_proj_half(FFN_NCH - 1, 1)

    out = acc_ref[...]
    if final_norm:
        out = _rmsnorm(out, fg_ref[...])

    @pl.when(n >= 2)
    def _():
        for c in out_copies(b, t, slot):
            c.wait()

    obuf_ref[slot] = out.reshape(G, SUBLANES, D_MODEL)
    for c in out_copies(b, t, slot):
        c.start()

    @pl.when(n == n_steps - 1)
    def _():
        for c in out_copies(b, t, slot):
            c.wait()
        if n_steps >= 2:
            for c in out_copies(b, t, 1 - slot):
                c.wait()


def _ffn(h, layer, g, w_up, conv_w, conv_b, w_down, final_g, *, final_norm):
    B, S, D = h.shape
    T, CH, NCH = FFN_T, FFN_CH, FFN_NCH
    G = T // SUBLANES
    n_tiles = S // T
    return pl.pallas_call(
        functools.partial(_ffn_kernel, layer=layer, final_norm=final_norm, n_tiles=n_tiles, n_steps=B * n_tiles),
        out_shape=jax.ShapeDtypeStruct((B, S, D), F32),
        grid=(B, n_tiles),
        in_specs=[pl.BlockSpec(memory_space=pl.ANY), _layer_block((1, D), layer),
                  pl.BlockSpec(memory_space=pl.ANY), _layer_block((FFN_TAPS, 2 * D_FF), layer),
                  _layer_block((1, 2 * D_FF), layer), pl.BlockSpec(memory_space=pl.ANY), _resident((1, D))],
        out_specs=pl.BlockSpec(memory_space=pl.ANY),
        scratch_shapes=[pltpu.VMEM((2, G, SUBLANES, D), F32),
                        pltpu.VMEM((2, G, SUBLANES, D), F32),
                        pltpu.VMEM((T, D), BF16),
                        pltpu.VMEM((T + FFN_PAD, 2 * CH), F32),
                        pltpu.VMEM((T + FFN_PAD, 2 * CH), F32),
                        pltpu.VMEM((T, CH), BF16),
                        pltpu.VMEM((T, CH), BF16),
                        pltpu.VMEM((NCH, FFN_PAD, 2 * CH), F32),
                        pltpu.VMEM((T, D), F32),
                        pltpu.VMEM((D, 2 * D_FF), BF16),
                        pltpu.VMEM((D_FF, D), BF16),
                        pltpu.VMEM((2, D, 2 * CH), F32),
                        pltpu.VMEM((2, CH, D), F32),
                        pltpu.SemaphoreType.DMA((2,)),
                        pltpu.SemaphoreType.DMA((2,)),
                        pltpu.SemaphoreType.DMA((2, 2))],
        compiler_params=pltpu.CompilerParams(dimension_semantics=("arbitrary", "arbitrary"),
                                             vmem_limit_bytes=VMEM_LIMIT_BYTES),
        name="ffn_final" if final_norm else "ffn",
    )(h, _rows(g), w_up, conv_w, _rows(conv_b), w_down, final_g.reshape(1, D))


EV_T = 512
MIX_W = 512
LRU_TAPS = 4
SC_TAPS = 3
LRU_DECAY_C = 8.0
GELU_C = 0.7978845608028654


def _gelu(x):
    return x * (0.5 * (1.0 + jnp.tanh(GELU_C * (x + 0.044715 * (x * x * x)))))


def _causal_taps(buf_ref, w_ref, b_ref, taps, T):
    w = w_ref[...]
    y = b_ref[...]
    for k in range(taps):
        off = HALO - (taps - 1) + k
        y = y + buf_ref[off:off + T, :] * w[k:k + 1, :]
    return y


def _even_kernel(h_ref, g_ref, win_ref, lcw_ref, lcb_ref, wg_ref, bg_ref, lam_ref, scw_ref, scb_ref, wout_ref,
                 o_ref, xn_ref, xa_ref, cv_ref, a_ref, hs_ref, y_ref, state_ref):
    T, W = EV_T, MIX_W

    @pl.when(pl.program_id(1) == 0)
    def _():
        xa_ref[0:HALO, :] = jnp.zeros((HALO, W), F32)
        cv_ref[0:HALO, :] = jnp.zeros((HALO, W), F32)
        state_ref[...] = jnp.zeros(state_ref.shape, F32)

    h = h_ref[0]
    xn_ref[...] = _rmsnorm(h, g_ref[...]).astype(BF16)

    def proj(c):
        return jnp.dot(xn_ref[...], win_ref[:, c * W:(c + 1) * W], preferred_element_type=F32)

    xa_ref[HALO:HALO + T, :] = proj(0)
    xc = _causal_taps(xa_ref, lcw_ref, lcb_ref, LRU_TAPS, T)
    xa_ref[0:HALO, :] = xa_ref[T:T + HALO, :]
    xcb = xc.astype(BF16)
    HW = W // 2
    pre = [jnp.dot(xcb[:, s * HW:(s + 1) * HW], wg_ref[s], preferred_element_type=F32) for s in range(2)]
    bg = bg_ref[...]
    r = _sigmoid(jnp.concatenate([pre[0][:, :HW], pre[1][:, :HW]], axis=1) + bg[:, :W])
    i = _sigmoid(jnp.concatenate([pre[0][:, HW:], pre[1][:, HW:]], axis=1) + bg[:, W:])
    lam = lam_ref[...]
    softplus_neg_lam = jnp.maximum(-lam, 0.0) + jnp.log1p(jnp.exp(-jnp.abs(lam)))
    log_a = (-LRU_DECAY_C * r) * softplus_neg_lam
    a = jnp.exp(log_a)
    th = jnp.tanh(log_a)
    u = jnp.sqrt((-2.0 * th) / (1.0 - th)) * (i * xc)

    G = T // 8
    a3 = a.reshape(G, 8, W)
    u3 = u.reshape(G, 8, W)
    row = lax.broadcasted_iota(jnp.int32, (G, 8, W), 1)
    for d in (1, 2, 4):
        keep = row >= d
        u3 = u3 + a3 * jnp.where(keep, pltpu.roll(u3, d, axis=1), 0.0)
        a3 = a3 * jnp.where(keep, pltpu.roll(a3, d, axis=1), 1.0)
    a_ref[...] = a3.reshape(T, W)
    hs_ref[...] = u3.reshape(T, W)
    carry = state_ref[0:1, :]
    for gi in range(G):
        rows = slice(gi * 8, gi * 8 + 8)
        hg = hs_ref[rows, :] + a_ref[rows, :] * carry
        hs_ref[rows, :] = hg
        carry = hg[7:8, :]
    state_ref[0:1, :] = carry
    y_ref[:, 0:W] = (hs_ref[...] * _gelu(proj(1))).astype(BF16)

    cv_ref[HALO:HALO + T, :] = proj(2) * proj(4)
    cc = _causal_taps(cv_ref, scw_ref, scb_ref, SC_TAPS, T)
    cv_ref[0:HALO, :] = cv_ref[T:T + HALO, :]
    y_ref[:, W:2 * W] = (proj(3) * cc).astype(BF16)

    o_ref[0] = h + jnp.dot(y_ref[...], wout_ref[...], preferred_element_type=F32)


def _block_diag(w):
    H, d, _ = w.shape
    return jnp.einsum('hij,hg->higj', w, jnp.eye(H, dtype=w.dtype)).reshape(H * d, H * d)


def _even_mixer(h, layer, g, w_in, lcw, lcb, wa, ba, wx, bx, lam, scw, scb, w_out):
    B, S, D = h.shape
    T, W = EV_T, MIX_W
    nh = wa.shape[1] // 2
    wg = jnp.stack([jnp.concatenate([_block_diag(wa[layer, s * nh:(s + 1) * nh]),
                                     _block_diag(wx[layer, s * nh:(s + 1) * nh])], axis=1) for s in range(2)])
    wg = wg.astype(BF16)
    bg = jnp.concatenate([ba[layer], bx[layer]]).reshape(1, 2 * W)
    tile = pl.BlockSpec((1, T, D), lambda b, t: (b, t, 0))
    return pl.pallas_call(
        _even_kernel,
        out_shape=jax.ShapeDtypeStruct((B, S, D), F32),
        grid=(B, S // T),
        in_specs=[tile, _layer_block((1, D), layer), _layer_block((D, 5 * W), layer),
                  _layer_block((LRU_TAPS, W), layer), _layer_block((1, W), layer),
                  _resident((2, W // 2, W)), _resident((1, 2 * W)), _layer_block((1, W), layer),
                  _layer_block((SC_TAPS, W), layer), _layer_block((1, W), layer), _layer_block((2 * W, D), layer)],
        out_specs=tile,
        scratch_shapes=[pltpu.VMEM((T, D), BF16),
                        pltpu.VMEM((T + HALO, W), F32),
                        pltpu.VMEM((T + HALO, W), F32),
                        pltpu.VMEM((T, W), F32),
                        pltpu.VMEM((T, W), F32),
                        pltpu.VMEM((T, 2 * W), BF16),
                        pltpu.VMEM((8, W), F32)],
        compiler_params=pltpu.CompilerParams(dimension_semantics=("arbitrary", "arbitrary"),
                                             vmem_limit_bytes=VMEM_LIMIT_BYTES),
        name="even_mixer",
    )(h, _rows(g), w_in, lcw, _rows(lcb), wg, bg, _rows(lam), scw, _rows(scb), w_out)


OD_T = 512
SGU_CHUNK = 128
N_HEADS = 8
HEAD_DIM = 64
F_PAD = 128
HEADS_PER_HALF = 4
QK_SCALE = HEAD_DIM ** -0.5
LOG2E = 1.4426950408889634
PAIR_W = 2 * HEAD_DIM
N_PAIRS = N_HEADS // 2


def _odd_in_kernel(h_ref, g_ref, win_ref, gn_ref, seg_ref, sw_ref, sb_ref, bf_ref,
                   yc_ref, q_ref, k_ref, v_ref, st_ref, xn_ref, gv_ref, u_ref, csum_ref):
    T, W = OD_T, MIX_W

    @pl.when(pl.program_id(1) == 0)
    def _():
        csum_ref[...] = jnp.zeros(csum_ref.shape, F32)

    xn_ref[...] = _rmsnorm(h_ref[0], g_ref[...]).astype(BF16)

    def proj(lo, width):
        return jnp.dot(xn_ref[...], win_ref[:, lo:lo + width], preferred_element_type=F32)

    fl = proj(5 * W, F_PAD) + bf_ref[...]
    logf = jnp.minimum(fl, 0.0) - jnp.log1p(jnp.exp(-jnp.abs(fl)))
    tri = (lax.broadcasted_iota(jnp.int32, (T, T), 1) <= lax.broadcasted_iota(jnp.int32, (T, T), 0)).astype(BF16)
    logf_hi = logf.astype(BF16)
    logf_mid = (logf - logf_hi.astype(F32)).astype(BF16)
    c_parts = jnp.dot(tri, jnp.concatenate([logf_hi, logf_mid], axis=1), preferred_element_type=F32)
    c_tile = c_parts[:, :F_PAD] + c_parts[:, F_PAD:] + csum_ref[0:1, :]
    csum_ref[0:1, :] = c_tile[T - 1:T, :]

    lane = lax.broadcasted_iota(jnp.int32, (T, PAIR_W), 1)
    low = lane < HEAD_DIM
    pq = proj(2 * W, W) * (QK_SCALE * LOG2E)
    pk = proj(3 * W, W)
    pv = proj(4 * W, W)
    for hh in range(N_HEADS):
        blk = slice((hh // 2) * PAIR_W, (hh // 2 + 1) * PAIR_W)
        own = low if hh % 2 == 0 else jnp.logical_not(low)
        a0 = HEAD_DIM if hh % 2 == 0 else 0
        x = jnp.broadcast_to(c_tile[:, hh:hh + 1] * (-LOG2E), (T, PAIR_W))
        hi = x.astype(BF16).astype(F32)
        r1 = x - hi
        mid = r1.astype(BF16).astype(F32)
        lo = r1 - mid
        c_parts = jnp.where(lane == a0, hi, jnp.where(lane == a0 + 1, mid, jnp.where(lane == a0 + 2, lo, 0.0)))
        ones3 = jnp.where((lane >= a0) & (lane < a0 + 3), 1.0, 0.0)
        one1 = jnp.where(lane == a0, 1.0, 0.0)
        q_ref[0, hh] = jnp.where(own, pq[:, blk], ones3).astype(BF16)
        k_ref[0, hh] = jnp.where(own, pk[:, blk], c_parts).astype(BF16)
        v_ref[0, hh] = jnp.where(own, pv[:, blk], one1).astype(BF16)

    def max_block_norm(x):
        sq = [jnp.max(jnp.sum(x[:, p * PAIR_W:(p + 1) * PAIR_W] ** 2, axis=-1, keepdims=True), axis=0, keepdims=True)
              for p in range(N_PAIRS)]
        return jnp.broadcast_to(jnp.sqrt(functools.reduce(jnp.maximum, sq)), (1, PAIR_W))

    st_ref[0, 0] = jnp.concatenate([c_tile[0:1, :], c_tile[T - 1:T, :], max_block_norm(pq), max_block_norm(pk),
                                    c_tile[T // 2 - 1:T // 2, :], jnp.zeros((3, PAIR_W), F32)], axis=0)

    u_ref[...] = _gelu(proj(0, W))
    g = _gelu(proj(W, W))
    ms = jnp.dot((g * g).astype(BF16), seg_ref[...], preferred_element_type=F32) * (1.0 / HEAD_DIM)
    gv_ref[...] = (g * lax.rsqrt(ms + EPS) * gn_ref[...]).astype(BF16)

    HW = HEADS_PER_HALF * HEAD_DIM
    lane_head = lax.broadcasted_iota(jnp.int32, (SGU_CHUNK, HW), 1) // HEAD_DIM
    wrow = lax.broadcasted_iota(jnp.int32, (SGU_CHUNK, HEADS_PER_HALF * SGU_CHUNK), 0)
    wcol = lax.broadcasted_iota(jnp.int32, (SGU_CHUNK, HEADS_PER_HALF * SGU_CHUNK), 1) % SGU_CHUNK
    for half in range(2):
        cols = slice(half * HW, (half + 1) * HW)
        w_causal = jnp.where(wcol <= wrow, sw_ref[half], 0.0).astype(BF16)
        bias = sb_ref[:, cols]
        for c in range(T // SGU_CHUNK):
            rows = slice(c * SGU_CHUNK, (c + 1) * SGU_CHUNK)
            gvc = gv_ref[rows, cols]
            stacked = jnp.concatenate(
                [jnp.where(lane_head == j, gvc, jnp.zeros_like(gvc)) for j in range(HEADS_PER_HALF)], axis=0)
            mixed = jnp.dot(w_causal, stacked, preferred_element_type=F32) + bias
            yc_ref[0, rows, cols] = (u_ref[rows, cols] * mixed).astype(BF16)


def _odd_in(h, layer, g, w_in, sgu_norm, sgu_w, sgu_b, fox_bf):
    B, S, D = h.shape
    T, W = OD_T, MIX_W
    n_in = 5 * W + F_PAD
    win = jnp.pad(w_in, ((0, 0), (0, 0), (0, n_in - w_in.shape[2]))).astype(BF16)
    seg = _block_diag(jnp.ones((N_HEADS, HEAD_DIM, HEAD_DIM), F32)).astype(BF16)
    sw = sgu_w[layer].reshape(2, HEADS_PER_HALF, SGU_CHUNK, SGU_CHUNK).transpose(0, 2, 1, 3).reshape(
        2, SGU_CHUNK, HEADS_PER_HALF * SGU_CHUNK)
    sb = jnp.repeat(sgu_b[layer].T, HEAD_DIM, axis=1)
    bf = jnp.pad(fox_bf[layer], (0, F_PAD - N_HEADS)).reshape(1, F_PAD)
    tile = pl.BlockSpec((1, T, D), lambda b, t: (b, t, 0))
    half_tile = pl.BlockSpec((1, T, W), lambda b, t: (b, t, 0))
    head_tile = pl.BlockSpec((1, N_HEADS, T, PAIR_W), lambda b, t: (b, 0, t, 0))
    act = jax.ShapeDtypeStruct((B, S, W), BF16)
    per_head = jax.ShapeDtypeStruct((B, N_HEADS, S, PAIR_W), BF16)
    return pl.pallas_call(
        _odd_in_kernel,
        out_shape=(act, per_head, per_head, per_head, jax.ShapeDtypeStruct((B, S // T, 8, PAIR_W), F32)),
        grid=(B, S // T),
        in_specs=[tile, _layer_block((1, D), layer), _layer_block((D, n_in), layer), _layer_block((1, W), layer),
                  _resident((W, W)), _resident((2, SGU_CHUNK, HEADS_PER_HALF * SGU_CHUNK)),
                  _resident((SGU_CHUNK, W)), _resident((1, F_PAD))],
        out_specs=(half_tile, head_tile, head_tile, head_tile,
                   pl.BlockSpec((1, 1, 8, PAIR_W), lambda b, t: (b, t, 0, 0))),
        scratch_shapes=[pltpu.VMEM((T, D), BF16),
                        pltpu.VMEM((T, W), BF16),
                        pltpu.VMEM((T, W), F32),
                        pltpu.VMEM((N_HEADS, 128), F32)],
        compiler_params=pltpu.CompilerParams(dimension_semantics=("arbitrary", "arbitrary"),
                                             vmem_limit_bytes=VMEM_LIMIT_BYTES),
        name="odd_in",
    )(h, _rows(g), win, _rows(sgu_norm), seg, sw, sb, bf)


AT_T = 512
AT_K = 256
KEY_BLOCKS_PER_TILE = AT_T // AT_K
M_INIT = -1e30
QK_AHEAD = 2
SKIP_LOG2 = -160.0
NORM_SLACK = 1.05


def _first_needed_block(stats):
    KB = KEY_BLOCKS_PER_TILE
    batch, n = stats.shape[:2]
    c_first = stats[:, :, 0, :N_HEADS]
    c_end = jnp.stack([stats[:, :, 4, :N_HEADS], stats[:, :, 1, :N_HEADS]], axis=2).reshape(batch, KB * n, N_HEADS)
    q_norm = stats[:, :, 2, 0]
    k_norm = jnp.max(stats[:, :, 3, 0], axis=1, keepdims=True)
    spread = NORM_SLACK * 2.0 * q_norm * k_norm
    decay = (c_end[:, None, :, :] - c_first[:, :, None, :]) * LOG2E
    block = lax.broadcasted_iota(jnp.int32, (n, KB * n), 1)
    tile = lax.broadcasted_iota(jnp.int32, (n, KB * n), 0)
    worst = jnp.max(spread[:, :, None, None] - decay, axis=-1)
    needed = (worst >= SKIP_LOG2) | (block >= KB * tile)[None]
    return jnp.min(jnp.where(needed, block[None], KB * n), axis=-1).astype(jnp.int32)


def _attn_kernel(first_ref, q_ref, kprev_ref, kcur_ref, vprev_ref, vcur_ref, k_hbm, v_hbm, yc_ref, h_ref, wout_ref,
                 o_ref, s_ref, mx_ref, m_ref, acc_ref, yd_ref, kfar_ref, vfar_ref, sem_far):
    T, K, W, KB = AT_T, AT_K, MIX_W, KEY_BLOCKS_PER_TILE
    qi = pl.program_id(1)

    def run_blocks(blocks, start=False):
        assert not start or blocks[0][2] == 0
        items = [(blk, hh) for blk in blocks for hh in range(N_HEADS)]

        def logits(item):
            (keys, _, r0, mask), hh = item
            rows = slice(r0, T)
            s = lax.dot_general(q_ref[0, hh, rows, :], keys(hh), (((1,), (1,)), ((), ())),
                                preferred_element_type=F32)
            if mask == 'diagonal':
                visible = (lax.broadcasted_iota(jnp.int32, (T - r0, K), 1)
                           <= lax.broadcasted_iota(jnp.int32, (T - r0, K), 0))
                s = jnp.where(visible, s, -jnp.inf)
            elif mask == 'if_not_first_tile':
                s = jnp.where(qi > 0, s, -jnp.inf)
            s_ref[hh, rows, :] = s
            mx_ref[hh, rows, :] = jnp.broadcast_to(jnp.max(s, axis=-1, keepdims=True), (T - r0, PAIR_W))

        def update(item, init):
            (_, values, r0, _), hh = item
            rows = slice(r0, T)
            if init:
                m_new = jnp.maximum(mx_ref[hh, rows, :], M_INIT)
            else:
                m_old = m_ref[hh, rows, :]
                m_new = jnp.maximum(m_old, mx_ref[hh, rows, :])
            p = jnp.exp2(s_ref[hh, rows, :] - jnp.concatenate([m_new] * (K // PAIR_W), axis=1))
            pv = jnp.dot(p.astype(BF16), values(hh), preferred_element_type=F32)
            m_ref[hh, rows, :] = m_new
            acc_ref[hh, rows, :] = pv if init else jnp.exp2(m_old - m_new) * acc_ref[hh, rows, :] + pv

        for item in items[:QK_AHEAD]:
            logits(item)
        for i, item in enumerate(items):
            if i + QK_AHEAD < len(items):
                logits(items[i + QK_AHEAD])
            update(item, init=start and i < N_HEADS)

    def tile_half(k_ref, v_ref, half):
        rows = slice(half * K, (half + 1) * K)
        return (lambda hh: k_ref[0, hh, rows, :]), (lambda hh: v_ref[0, hh, rows, :])

    def far_block(kb, carry):
        bb = pl.program_id(0)
        k0 = pl.multiple_of(kb * K, K)
        copies = [pltpu.make_async_copy(k_hbm.at[bb, :, pl.ds(k0, K), :], kfar_ref, sem_far.at[0]),
                  pltpu.make_async_copy(v_hbm.at[bb, :, pl.ds(k0, K), :], vfar_ref, sem_far.at[1])]
        for c in copies:
            c.start()
        for c in copies:
            c.wait()
        run_blocks([((lambda hh: kfar_ref[hh]), (lambda hh: vfar_ref[hh]), 0, None)])
        return carry

    run_blocks([tile_half(kprev_ref, vprev_ref, KB - 1) + (0, 'if_not_first_tile')]
               + [tile_half(kcur_ref, vcur_ref, d) + (d * K, 'diagonal') for d in range(KB)], start=True)
    lax.fori_loop(first_ref[pl.program_id(0), qi], KB * qi - 1, far_block, 0)

    low = lax.broadcasted_iota(jnp.int32, (T, PAIR_W), 1) < HEAD_DIM
    for pair in range(N_PAIRS):
        acc_a = acc_ref[2 * pair]
        acc_b = acc_ref[2 * pair + 1]
        y = jnp.where(low, acc_a / acc_a[:, HEAD_DIM:HEAD_DIM + 1], acc_b / acc_b[:, 0:1])
        yd_ref[:, pair * PAIR_W:(pair + 1) * PAIR_W] = y.astype(BF16)
    o_ref[0] = (h_ref[0]
                + jnp.dot(yc_ref[0], wout_ref[0:W, :], preferred_element_type=F32)
                + jnp.dot(yd_ref[...], wout_ref[W:2 * W, :], preferred_element_type=F32))


def _attention(q, k, v, stats, yc, h, layer, w_out):
    B, S, D = h.shape
    T, K, W = AT_T, AT_K, MIX_W
    assert OD_T == AT_T and KEY_BLOCKS_PER_TILE == 2, "tile summaries hold c at the middle and end of an odd_in tile"
    q_tile = pl.BlockSpec((1, N_HEADS, T, PAIR_W), lambda b, t: (b, 0, t, 0))
    prev_tile = pl.BlockSpec((1, N_HEADS, T, PAIR_W), lambda b, t: (b, 0, jnp.maximum(t - 1, 0), 0))
    hbm = pl.BlockSpec(memory_space=pl.ANY)
    tile = pl.BlockSpec((1, T, D), lambda b, t: (b, t, 0))
    return pl.pallas_call(
        _attn_kernel,
        out_shape=jax.ShapeDtypeStruct((B, S, D), F32),
        grid=(B, S // T),
        in_specs=[pl.BlockSpec(memory_space=pltpu.SMEM), q_tile, prev_tile, q_tile, prev_tile, q_tile, hbm, hbm,
                  pl.BlockSpec((1, T, W), lambda b, t: (b, t, 0)), tile, _layer_block((2 * W, D), layer)],
        out_specs=tile,
        scratch_shapes=[pltpu.VMEM((N_HEADS, T, K), F32),
                        pltpu.VMEM((N_HEADS, T, PAIR_W), F32),
                        pltpu.VMEM((N_HEADS, T, PAIR_W), F32),
                        pltpu.VMEM((N_HEADS, T, PAIR_W), F32),
                        pltpu.VMEM((T, W), BF16),
                        pltpu.VMEM((N_HEADS, K, PAIR_W), BF16),
                        pltpu.VMEM((N_HEADS, K, PAIR_W), BF16),
                        pltpu.SemaphoreType.DMA((2,))],
        compiler_params=pltpu.CompilerParams(dimension_semantics=("arbitrary", "arbitrary"),
                                             vmem_limit_bytes=VMEM_LIMIT_BYTES),
        name="fox_attention",
    )(_first_needed_block(stats), q, k, k, v, v, k, v, yc, h, w_out)


def kernel(x, mix0_norm, mix0_w_in, lru_conv_w, lru_conv_b, lru_wa, lru_ba, lru_wx, lru_bx, lru_lambda,
           sconv_w, sconv_b, mix0_w_out, mix1_norm, mix1_w_in, sgu_norm, sgu_w, sgu_b, fox_bf, mix1_w_out,
           ffn_norm, ffn_up, ffn_conv_w, ffn_conv_b, ffn_down, final_norm):
    h = _even_mixer(x, 0, mix0_norm, mix0_w_in.astype(BF16), lru_conv_w, lru_conv_b, lru_wa, lru_ba, lru_wx, lru_bx,
                    lru_lambda, sconv_w, sconv_b, mix0_w_out.astype(BF16))
    h = _ffn(h, 0, ffn_norm, ffn_up, ffn_conv_w, ffn_conv_b, ffn_down, final_norm, final_norm=False)
    yc, q, k, v, stats = _odd_in(h, 0, mix1_norm, mix1_w_in, sgu_norm, sgu_w, sgu_b, fox_bf)
    h = _attention(q, k, v, stats, yc, h, 0, mix1_w_out.astype(BF16))
    return _ffn(h, 1, ffn_norm, ffn_up, ffn_conv_w, ffn_conv_b, ffn_down, final_norm, final_norm=True)
```

```python
import functools

import jax
import jax.numpy as jnp
from jax import lax
from jax.experimental import pallas as pl
from jax.experimental.pallas import tpu as pltpu

F32 = jnp.float32
BF16 = jnp.bfloat16

EPS = 1e-6
D_MODEL = 1024
D_FF = 2816
FFN_TAPS = 3
HALO = 8
VMEM_LIMIT_BYTES = 56 * 1024 * 1024


def _rmsnorm(x, g):
    ms = jnp.mean(x * x, axis=-1, keepdims=True)
    return x * lax.rsqrt(ms + EPS) * g


def _sigmoid(x):
    return 1.0 / (1.0 + jnp.exp(-x))


def _resident(shape):
    return pl.BlockSpec(shape, lambda *_: (0,) * len(shape), pipeline_mode=pl.Buffered(1))


def _layer_block(shape, layer):
    return pl.BlockSpec((None,) + tuple(shape), lambda *_: (layer,) + (0,) * len(shape),
                        pipeline_mode=pl.Buffered(1))


def _layer_row(p, layer):
    if p.shape[0] == 1:
        return _resident(p.shape), p
    return _layer_block((1, p.shape[1]), layer), p[:, None, :]


def _stream_to_bf16(srcs, dsts, stage_ref, sems):
    copies = [pltpu.make_async_copy(src, stage_ref.at[i % 2], sems.at[i % 2]) for i, src in enumerate(srcs)]
    copies[0].start()
    for i, dst in enumerate(dsts):
        if i + 1 < len(copies):
            copies[i + 1].start()
        copies[i].wait()
        dst[...] = stage_ref[i % 2].astype(BF16)


FFN_T = 512
FFN_CH = 256
FFN_NCH = D_FF // FFN_CH
SUBLANES = 8
FFN_PAD = (FFN_TAPS - 1) * SUBLANES


def _ffn_kernel(h_hbm, g_ref, wup_hbm, cw_ref, cb_ref, wdn_hbm, fg_ref, o_hbm,
                hbuf_ref, obuf_ref, xn_ref, work0_ref, work1_ref, act0_ref, act1_ref, halo_ref, acc_ref,
                wup_ref, wdn_ref, stage_up_ref, stage_dn_ref, sem_in, sem_out, sem_w,
                *, layer, final_norm, n_tiles, n_steps):
    T, CH, PAD = FFN_T, FFN_CH, FFN_PAD
    G = T // SUBLANES
    b = pl.program_id(0)
    t = pl.program_id(1)
    n = b * n_tiles + t
    slot = lax.rem(n, 2)

    @pl.when(n == 0)
    def _():
        UC = stage_up_ref.shape[2]
        DR = stage_dn_ref.shape[1]
        _stream_to_bf16([wup_hbm.at[layer, :, pl.ds(c * UC, UC)] for c in range(wup_ref.shape[1] // UC)],
                        [wup_ref.at[:, pl.ds(c * UC, UC)] for c in range(wup_ref.shape[1] // UC)],
                        stage_up_ref, sem_w.at[0])
        _stream_to_bf16([wdn_hbm.at[layer, pl.ds(c * DR, DR), :] for c in range(wdn_ref.shape[0] // DR)],
                        [wdn_ref.at[pl.ds(c * DR, DR), :] for c in range(wdn_ref.shape[0] // DR)],
                        stage_dn_ref, sem_w.at[1])

    def in_copies(bb, tt, sl):
        return [pltpu.make_async_copy(h_hbm.at[bb, pl.ds(tt * T + r * G, G), :], hbuf_ref.at[sl, :, r, :],
                                      sem_in.at[sl]) for r in range(SUBLANES)]

    def out_copies(bb, tt, sl):
        return [pltpu.make_async_copy(obuf_ref.at[sl, :, r, :], o_hbm.at[bb, pl.ds(tt * T + r * G, G), :],
                                      sem_out.at[sl]) for r in range(SUBLANES)]

    @pl.when(n == 0)
    def _():
        for c in in_copies(b, t, slot):
            c.start()

    @pl.when(n + 1 < n_steps)
    def _():
        n1 = n + 1
        for c in in_copies(n1 // n_tiles, lax.rem(n1, n_tiles), 1 - slot):
            c.start()

    @pl.when(t == 0)
    def _():
        halo_ref[...] = jnp.zeros(halo_ref.shape, F32)

    for c in in_copies(b, t, slot):
        c.wait()
    h = hbuf_ref[slot].reshape(T, D_MODEL)
    xn_ref[...] = _rmsnorm(h, g_ref[layer:layer + 1, :]).astype(BF16)
    acc_ref[...] = h

    works = (work0_ref, work1_ref)
    acts = (act0_ref, act1_ref)
    first_row = lax.broadcasted_iota(jnp.int32, (SUBLANES, CH), 0) == 0

    def chunk_cols(j):
        return slice(j * CH, (j + 1) * CH), slice(D_FF + j * CH, D_FF + (j + 1) * CH)

    def up_proj_half(j, half):
        work_ref = works[j % 2]
        hcols = slice(half * CH, (half + 1) * CH)
        work_ref[PAD:PAD + T, hcols] = jnp.dot(xn_ref[...], wup_ref[:, chunk_cols(j)[half]],
                                               preferred_element_type=F32)
        for k in range(FFN_TAPS - 1):
            rows = slice(k * SUBLANES, (k + 1) * SUBLANES)
            tail = pltpu.roll(work_ref[T + k * SUBLANES:T + (k + 1) * SUBLANES, hcols], 1, axis=0)
            work_ref[rows, hcols] = jnp.where(first_row, halo_ref[j, rows, hcols], tail)
            halo_ref[j, rows, hcols] = tail

    def down_proj_half(j, half):
        cols = slice(half * (D_MODEL // 2), (half + 1) * (D_MODEL // 2))
        acc_ref[:, cols] += jnp.dot(acts[j % 2][...], wdn_ref[j * CH:(j + 1) * CH, cols],
                                    preferred_element_type=F32)

    up_proj_half(0, 0)
    up_proj_half(0, 1)
    for j in range(FFN_NCH):
        for half in range(2):
            if j + 1 < FFN_NCH:
                up_proj_half(j + 1, half)
            if j >= 1:
                down_proj_half(j - 1, half)
        work_ref = works[j % 2]
        gate_cols, val_cols = chunk_cols(j)
        cw = jnp.concatenate([cw_ref[:, gate_cols], cw_ref[:, val_cols]], axis=1)
        y = jnp.concatenate([cb_ref[layer:layer + 1, gate_cols], cb_ref[layer:layer + 1, val_cols]], axis=1)
        for k in range(FFN_TAPS):
            y = y + work_ref[k * SUBLANES:k * SUBLANES + T, :] * cw[k:k + 1, :]
        gate = y[:, :CH]
        val = y[:, CH:]
        acts[j % 2][...] = (gate * _sigmoid(gate) * val).astype(BF16)
    down_proj_half(FFN_NCH - 1, 0)
    down_proj_half(FFN_NCH - 1, 1)

    out = acc_ref[...]
    if final_norm:
        out = _rmsnorm(out, fg_ref[...])

    @pl.when(n >= 2)
    def _():
        for c in out_copies(b, t, slot):
            c.wait()

    obuf_ref[slot] = out.reshape(G, SUBLANES, D_MODEL)
    for c in out_copies(b, t, slot):
        c.start()

    @pl.when(n == n_steps - 1)
    def _():
        for c in out_copies(b, t, slot):
            c.wait()
        if n_steps >= 2:
            for c in out_copies(b, t, 1 - slot):
                c.wait()


def _ffn(h, layer, g, w_up, conv_w, conv_b, w_down, final_g, *, final_norm):
    B, S, D = h.shape
    T, CH, NCH = FFN_T, FFN_CH, FFN_NCH
    G = T // SUBLANES
    n_tiles = S // T
    return pl.pallas_call(
        functools.partial(_ffn_kernel, layer=layer, final_norm=final_norm, n_tiles=n_tiles, n_steps=B * n_tiles),
        out_shape=jax.ShapeDtypeStruct((B, S, D), F32),
        grid=(B, n_tiles),
        in_specs=[pl.BlockSpec(memory_space=pl.ANY), _resident(g.shape),
                  pl.BlockSpec(memory_space=pl.ANY), _layer_block((FFN_TAPS, 2 * D_FF), layer),
                  _resident(conv_b.shape), pl.BlockSpec(memory_space=pl.ANY), _resident((1, D))],
        out_specs=pl.BlockSpec(memory_space=pl.ANY),
        scratch_shapes=[pltpu.VMEM((2, G, SUBLANES, D), F32),
                        pltpu.VMEM((2, G, SUBLANES, D), F32),
                        pltpu.VMEM((T, D), BF16),
                        pltpu.VMEM((T + FFN_PAD, 2 * CH), F32),
                        pltpu.VMEM((T + FFN_PAD, 2 * CH), F32),
                        pltpu.VMEM((T, CH), BF16),
                        pltpu.VMEM((T, CH), BF16),
                        pltpu.VMEM((NCH, FFN_PAD, 2 * CH), F32),
                        pltpu.VMEM((T, D), F32),
                        pltpu.VMEM((D, 2 * D_FF), BF16),
                        pltpu.VMEM((D_FF, D), BF16),
                        pltpu.VMEM((2, D, 2 * CH), F32),
                        pltpu.VMEM((2, CH, D), F32),
                        pltpu.SemaphoreType.DMA((2,)),
                        pltpu.SemaphoreType.DMA((2,)),
                        pltpu.SemaphoreType.DMA((2, 2))],
        compiler_params=pltpu.CompilerParams(dimension_semantics=("arbitrary", "arbitrary"),
                                             vmem_limit_bytes=VMEM_LIMIT_BYTES),
        name="ffn_final" if final_norm else "ffn",
    )(h, g, w_up, conv_w, conv_b, w_down, final_g.reshape(1, D))


EV_T = 512
MIX_W = 512
LRU_TAPS = 4
SC_TAPS = 3
LRU_DECAY_C = 8.0
GELU_C = 0.7978845608028654


def _gelu(x):
    return x * (0.5 * (1.0 + jnp.tanh(GELU_C * (x + 0.044715 * (x * x * x)))))


def _causal_taps(buf_ref, w_ref, b_ref, taps, T):
    w = w_ref[...]
    y = b_ref[...]
    for k in range(taps):
        off = HALO - (taps - 1) + k
        y = y + buf_ref[off:off + T, :] * w[k:k + 1, :]
    return y


def _even_kernel(h_ref, g_ref, win_hbm, lcw_ref, lcb_ref, wg_ref, bg_ref, lam_ref, scw_ref, scb_ref, wout_hbm,
                 o_ref, xn_ref, xa_ref, cv_ref, a_ref, hs_ref, y_ref, state_ref,
                 win_ref, wout_ref, stage_in_ref, stage_out_ref, sem_w, *, layer):
    T, W = EV_T, MIX_W

    @pl.when((pl.program_id(0) == 0) & (pl.program_id(1) == 0))
    def _():
        _stream_to_bf16([win_hbm.at[layer, :, pl.ds(c * W, W)] for c in range(win_ref.shape[1] // W)],
                        [win_ref.at[:, pl.ds(c * W, W)] for c in range(win_ref.shape[1] // W)],
                        stage_in_ref, sem_w.at[0])
        _stream_to_bf16([wout_hbm.at[layer, pl.ds(c * W, W), :] for c in range(wout_ref.shape[0] // W)],
                        [wout_ref.at[pl.ds(c * W, W), :] for c in range(wout_ref.shape[0] // W)],
                        stage_out_ref, sem_w.at[1])

    @pl.when(pl.program_id(1) == 0)
    def _():
        xa_ref[0:HALO, :] = jnp.zeros((HALO, W), F32)
        cv_ref[0:HALO, :] = jnp.zeros((HALO, W), F32)
        state_ref[...] = jnp.zeros(state_ref.shape, F32)

    h = h_ref[0]
    xn_ref[...] = _rmsnorm(h, g_ref[...]).astype(BF16)

    def proj(c):
        return jnp.dot(xn_ref[...], win_ref[:, c * W:(c + 1) * W], preferred_element_type=F32)

    xa_ref[HALO:HALO + T, :] = proj(0)
    xc = _causal_taps(xa_ref, lcw_ref, lcb_ref, LRU_TAPS, T)
    xa_ref[0:HALO, :] = xa_ref[T:T + HALO, :]
    xcb = xc.astype(BF16)
    HW = W // 2
    pre = [jnp.dot(xcb[:, s * HW:(s + 1) * HW], wg_ref[s], preferred_element_type=F32) for s in range(2)]
    bg = bg_ref[...]
    r = _sigmoid(jnp.concatenate([pre[0][:, :HW], pre[1][:, :HW]], axis=1) + bg[:, :W])
    i = _sigmoid(jnp.concatenate([pre[0][:, HW:], pre[1][:, HW:]], axis=1) + bg[:, W:])
    lam = lam_ref[...]
    softplus_neg_lam = jnp.maximum(-lam, 0.0) + jnp.log1p(jnp.exp(-jnp.abs(lam)))
    log_a = (-LRU_DECAY_C * r) * softplus_neg_lam
    a = jnp.exp(log_a)
    th = jnp.tanh(log_a)
    u = jnp.sqrt((-2.0 * th) / (1.0 - th)) * (i * xc)

    G = T // 8
    a3 = a.reshape(G, 8, W)
    u3 = u.reshape(G, 8, W)
    row = lax.broadcasted_iota(jnp.int32, (G, 8, W), 1)
    for d in (1, 2, 4):
        keep = row >= d
        u3 = u3 + a3 * jnp.where(keep, pltpu.roll(u3, d, axis=1), 0.0)
        a3 = a3 * jnp.where(keep, pltpu.roll(a3, d, axis=1), 1.0)
    a_ref[...] = a3.reshape(T, W)
    hs_ref[...] = u3.reshape(T, W)
    carry = state_ref[0:1, :]
    for gi in range(G):
        rows = slice(gi * 8, gi * 8 + 8)
        hg = hs_ref[rows, :] + a_ref[rows, :] * carry
        hs_ref[rows, :] = hg
        carry = hg[7:8, :]
    state_ref[0:1, :] = carry
    y_ref[:, 0:W] = (hs_ref[...] * _gelu(proj(1))).astype(BF16)

    cv_ref[HALO:HALO + T, :] = proj(2) * proj(4)
    cc = _causal_taps(cv_ref, scw_ref, scb_ref, SC_TAPS, T)
    cv_ref[0:HALO, :] = cv_ref[T:T + HALO, :]
    y_ref[:, W:2 * W] = (proj(3) * cc).astype(BF16)

    o_ref[0] = h + jnp.dot(y_ref[...], wout_ref[...], preferred_element_type=F32)


def _block_diag(w):
    H, d, _ = w.shape
    return jnp.einsum('hij,hg->higj', w, jnp.eye(H, dtype=w.dtype)).reshape(H * d, H * d)


def _even_mixer(h, layer, g, w_in, lcw, lcb, wa, ba, wx, bx, lam, scw, scb, w_out):
    B, S, D = h.shape
    T, W = EV_T, MIX_W
    nh = wa.shape[1] // 2
    wg = jnp.stack([jnp.concatenate([_block_diag(wa[layer, s * nh:(s + 1) * nh]),
                                     _block_diag(wx[layer, s * nh:(s + 1) * nh])], axis=1) for s in range(2)])
    wg = wg.astype(BF16)
    bg = jnp.concatenate([ba[layer], bx[layer]]).reshape(1, 2 * W)
    tile = pl.BlockSpec((1, T, D), lambda b, t: (b, t, 0))
    (g_spec, g), (lcb_spec, lcb), (lam_spec, lam), (scb_spec, scb) = (_layer_row(p, layer) for p in (g, lcb, lam, scb))
    hbm = pl.BlockSpec(memory_space=pl.ANY)
    return pl.pallas_call(
        functools.partial(_even_kernel, layer=layer),
        out_shape=jax.ShapeDtypeStruct((B, S, D), F32),
        grid=(B, S // T),
        in_specs=[tile, g_spec, hbm, _layer_block((LRU_TAPS, W), layer), lcb_spec,
                  _resident((2, W // 2, W)), _resident((1, 2 * W)), lam_spec,
                  _layer_block((SC_TAPS, W), layer), scb_spec, hbm],
        out_specs=tile,
        scratch_shapes=[pltpu.VMEM((T, D), BF16),
                        pltpu.VMEM((T + HALO, W), F32),
                        pltpu.VMEM((T + HALO, W), F32),
                        pltpu.VMEM((T, W), F32),
                        pltpu.VMEM((T, W), F32),
                        pltpu.VMEM((T, 2 * W), BF16),
                        pltpu.VMEM((8, W), F32),
                        pltpu.VMEM((D, 5 * W), BF16),
                        pltpu.VMEM((2 * W, D), BF16),
                        pltpu.VMEM((2, D, W), F32),
                        pltpu.VMEM((2, W, D), F32),
                        pltpu.SemaphoreType.DMA((2, 2))],
        compiler_params=pltpu.CompilerParams(dimension_semantics=("arbitrary", "arbitrary"),
                                             vmem_limit_bytes=VMEM_LIMIT_BYTES),
        name="even_mixer",
    )(h, g, w_in, lcw, lcb, wg, bg, lam, scw, scb, w_out)


OD_T = 512
SGU_CHUNK = 128
N_HEADS = 8
HEAD_DIM = 64
F_PAD = 128
HEADS_PER_HALF = 4
QK_SCALE = HEAD_DIM ** -0.5
LOG2E = 1.4426950408889634
PAIR_W = 2 * HEAD_DIM
N_PAIRS = N_HEADS // 2


def _odd_in_kernel(h_ref, g_ref, win_ref, gn_ref, seg_ref, sw_ref, sb_ref, bf_ref,
                   yc_ref, q_ref, k_ref, v_ref, st_ref, xn_ref, gv_ref, u_ref, csum_ref):
    T, W = OD_T, MIX_W

    @pl.when(pl.program_id(1) == 0)
    def _():
        csum_ref[...] = jnp.zeros(csum_ref.shape, F32)

    xn_ref[...] = _rmsnorm(h_ref[0], g_ref[...]).astype(BF16)

    def proj(lo, width):
        return jnp.dot(xn_ref[...], win_ref[:, lo:lo + width], preferred_element_type=F32)

    fl = proj(5 * W, F_PAD) + bf_ref[...]
    logf = jnp.minimum(fl, 0.0) - jnp.log1p(jnp.exp(-jnp.abs(fl)))
    tri = (lax.broadcasted_iota(jnp.int32, (T, T), 1) <= lax.broadcasted_iota(jnp.int32, (T, T), 0)).astype(BF16)
    logf_hi = logf.astype(BF16)
    logf_mid = (logf - logf_hi.astype(F32)).astype(BF16)
    c_parts = jnp.dot(tri, jnp.concatenate([logf_hi, logf_mid], axis=1), preferred_element_type=F32)
    c_tile = c_parts[:, :F_PAD] + c_parts[:, F_PAD:] + csum_ref[0:1, :]
    csum_ref[0:1, :] = c_tile[T - 1:T, :]

    lane = lax.broadcasted_iota(jnp.int32, (T, PAIR_W), 1)
    low = lane < HEAD_DIM
    pq = proj(2 * W, W) * (QK_SCALE * LOG2E)
    pk = proj(3 * W, W)
    pv = proj(4 * W, W)
    for hh in range(N_HEADS):
        blk = slice((hh // 2) * PAIR_W, (hh // 2 + 1) * PAIR_W)
        own = low if hh % 2 == 0 else jnp.logical_not(low)
        a0 = HEAD_DIM if hh % 2 == 0 else 0
        x = jnp.broadcast_to(c_tile[:, hh:hh + 1] * (-LOG2E), (T, PAIR_W))
        hi = x.astype(BF16).astype(F32)
        r1 = x - hi
        mid = r1.astype(BF16).astype(F32)
        lo = r1 - mid
        c_parts = jnp.where(lane == a0, hi, jnp.where(lane == a0 + 1, mid, jnp.where(lane == a0 + 2, lo, 0.0)))
        ones3 = jnp.where((lane >= a0) & (lane < a0 + 3), 1.0, 0.0)
        one1 = jnp.where(lane == a0, 1.0, 0.0)
        q_ref[0, hh] = jnp.where(own, pq[:, blk], ones3).astype(BF16)
        k_ref[0, hh] = jnp.where(own, pk[:, blk], c_parts).astype(BF16)
        v_ref[0, hh] = jnp.where(own, pv[:, blk], one1).astype(BF16)

    def max_block_norm(x):
        sq = [jnp.max(jnp.sum(x[:, p * PAIR_W:(p + 1) * PAIR_W] ** 2, axis=-1, keepdims=True), axis=0, keepdims=True)
              for p in range(N_PAIRS)]
        return jnp.broadcast_to(jnp.sqrt(functools.reduce(jnp.maximum, sq)), (1, PAIR_W))

    st_ref[0, 0] = jnp.concatenate([c_tile[0:1, :], c_tile[T - 1:T, :], max_block_norm(pq), max_block_norm(pk),
                                    c_tile[T // 2 - 1:T // 2, :], jnp.zeros((3, PAIR_W), F32)], axis=0)

    u_ref[...] = _gelu(proj(0, W))
    g = _gelu(proj(W, W))
    ms = jnp.dot((g * g).astype(BF16), seg_ref[...], preferred_element_type=F32) * (1.0 / HEAD_DIM)
    gv_ref[...] = (g * lax.rsqrt(ms + EPS) * gn_ref[...]).astype(BF16)

    HW = HEADS_PER_HALF * HEAD_DIM
    lane_head = lax.broadcasted_iota(jnp.int32, (SGU_CHUNK, HW), 1) // HEAD_DIM
    wrow = lax.broadcasted_iota(jnp.int32, (SGU_CHUNK, HEADS_PER_HALF * SGU_CHUNK), 0)
    wcol = lax.broadcasted_iota(jnp.int32, (SGU_CHUNK, HEADS_PER_HALF * SGU_CHUNK), 1) % SGU_CHUNK
    for half in range(2):
        cols = slice(half * HW, (half + 1) * HW)
        w_causal = jnp.where(wcol <= wrow, sw_ref[half], 0.0).astype(BF16)
        bias = sb_ref[:, cols]
        for c in range(T // SGU_CHUNK):
            rows = slice(c * SGU_CHUNK, (c + 1) * SGU_CHUNK)
            gvc = gv_ref[rows, cols]
            stacked = jnp.concatenate(
                [jnp.where(lane_head == j, gvc, jnp.zeros_like(gvc)) for j in range(HEADS_PER_HALF)], axis=0)
            mixed = jnp.dot(w_causal, stacked, preferred_element_type=F32) + bias
            yc_ref[0, rows, cols] = (u_ref[rows, cols] * mixed).astype(BF16)


def _odd_in(h, layer, g, w_in, sgu_norm, sgu_w, sgu_b, fox_bf):
    B, S, D = h.shape
    T, W = OD_T, MIX_W
    n_in = 5 * W + F_PAD
    win = jnp.pad(w_in, ((0, 0), (0, 0), (0, n_in - w_in.shape[2]))).astype(BF16)
    seg = _block_diag(jnp.ones((N_HEADS, HEAD_DIM, HEAD_DIM), F32)).astype(BF16)
    sw = sgu_w[layer].reshape(2, HEADS_PER_HALF, SGU_CHUNK, SGU_CHUNK).transpose(0, 2, 1, 3).reshape(
        2, SGU_CHUNK, HEADS_PER_HALF * SGU_CHUNK)
    sb = jnp.repeat(sgu_b[layer].T, HEAD_DIM, axis=1)
    bf = jnp.pad(fox_bf[layer], (0, F_PAD - N_HEADS)).reshape(1, F_PAD)
    (g_spec, g), (gn_spec, sgu_norm) = _layer_row(g, layer), _layer_row(sgu_norm, layer)
    tile = pl.BlockSpec((1, T, D), lambda b, t: (b, t, 0))
    half_tile = pl.BlockSpec((1, T, W), lambda b, t: (b, t, 0))
    head_tile = pl.BlockSpec((1, N_HEADS, T, PAIR_W), lambda b, t: (b, 0, t, 0))
    act = jax.ShapeDtypeStruct((B, S, W), BF16)
    per_head = jax.ShapeDtypeStruct((B, N_HEADS, S, PAIR_W), BF16)
    return pl.pallas_call(
        _odd_in_kernel,
        out_shape=(act, per_head, per_head, per_head, jax.ShapeDtypeStruct((B, S // T, 8, PAIR_W), F32)),
        grid=(B, S // T),
        in_specs=[tile, g_spec, _layer_block((D, n_in), layer), gn_spec,
                  _resident((W, W)), _resident((2, SGU_CHUNK, HEADS_PER_HALF * SGU_CHUNK)),
                  _resident((SGU_CHUNK, W)), _resident((1, F_PAD))],
        out_specs=(half_tile, head_tile, head_tile, head_tile,
                   pl.BlockSpec((1, 1, 8, PAIR_W), lambda b, t: (b, t, 0, 0))),
        scratch_shapes=[pltpu.VMEM((T, D), BF16),
                        pltpu.VMEM((T, W), BF16),
                        pltpu.VMEM((T, W), F32),
                        pltpu.VMEM((N_HEADS, 128), F32)],
        compiler_params=pltpu.CompilerParams(dimension_semantics=("arbitrary", "arbitrary"),
                                             vmem_limit_bytes=VMEM_LIMIT_BYTES),
        name="odd_in",
    )(h, g, win, sgu_norm, seg, sw, sb, bf)


AT_T = 512
AT_K = 256
KEY_BLOCKS_PER_TILE = AT_T // AT_K
M_INIT = -1e30
QK_AHEAD = 2
SKIP_LOG2 = -160.0
NORM_SLACK = 1.05


def _first_needed_block(stats):
    KB = KEY_BLOCKS_PER_TILE
    batch, n = stats.shape[:2]
    c_first = stats[:, :, 0, :N_HEADS]
    c_end = jnp.stack([stats[:, :, 4, :N_HEADS], stats[:, :, 1, :N_HEADS]], axis=2).reshape(batch, KB * n, N_HEADS)
    q_norm = stats[:, :, 2, 0]
    k_norm = jnp.max(stats[:, :, 3, 0], axis=1, keepdims=True)
    spread = NORM_SLACK * 2.0 * q_norm * k_norm
    decay = (c_end[:, None, :, :] - c_first[:, :, None, :]) * LOG2E
    block = lax.broadcasted_iota(jnp.int32, (n, KB * n), 1)
    tile = lax.broadcasted_iota(jnp.int32, (n, KB * n), 0)
    worst = jnp.max(spread[:, :, None, None] - decay, axis=-1)
    needed = (worst >= SKIP_LOG2) | (block >= KB * tile)[None]
    return jnp.min(jnp.where(needed, block[None], KB * n), axis=-1).astype(jnp.int32)


def _attn_kernel(first_ref, q_ref, kprev_ref, kcur_ref, vprev_ref, vcur_ref, k_hbm, v_hbm, yc_ref, h_ref, wout_ref,
                 o_ref, s_ref, mx_ref, m_ref, acc_ref, yd_ref, kfar_ref, vfar_ref, sem_far):
    T, K, W, KB = AT_T, AT_K, MIX_W, KEY_BLOCKS_PER_TILE
    qi = pl.program_id(1)

    def run_blocks(blocks, start=False):
        assert not start or blocks[0][2] == 0
        items = [(blk, hh) for blk in blocks for hh in range(N_HEADS)]

        def logits(item):
            (keys, _, r0, mask), hh = item
            rows = slice(r0, T)
            s = lax.dot_general(q_ref[0, hh, rows, :], keys(hh), (((1,), (1,)), ((), ())),
                                preferred_element_type=F32)
            if mask == 'diagonal':
                visible = (lax.broadcasted_iota(jnp.int32, (T - r0, K), 1)
                           <= lax.broadcasted_iota(jnp.int32, (T - r0, K), 0))
                s = jnp.where(visible, s, -jnp.inf)
            elif mask == 'if_not_first_tile':
                s = jnp.where(qi > 0, s, -jnp.inf)
            s_ref[hh, rows, :] = s
            mx_ref[hh, rows, :] = jnp.broadcast_to(jnp.max(s, axis=-1, keepdims=True), (T - r0, PAIR_W))

        def update(item, init):
            (_, values, r0, _), hh = item
            rows = slice(r0, T)
            if init:
                m_new = jnp.maximum(mx_ref[hh, rows, :], M_INIT)
            else:
                m_old = m_ref[hh, rows, :]
                m_new = jnp.maximum(m_old, mx_ref[hh, rows, :])
            p = jnp.exp2(s_ref[hh, rows, :] - jnp.concatenate([m_new] * (K // PAIR_W), axis=1))
            pv = jnp.dot(p.astype(BF16), values(hh), preferred_element_type=F32)
            m_ref[hh, rows, :] = m_new
            acc_ref[hh, rows, :] = pv if init else jnp.exp2(m_old - m_new) * acc_ref[hh, rows, :] + pv

        for item in items[:QK_AHEAD]:
            logits(item)
        for i, item in enumerate(items):
            if i + QK_AHEAD < len(items):
                logits(items[i + QK_AHEAD])
            update(item, init=start and i < N_HEADS)

    def tile_half(k_ref, v_ref, half):
        rows = slice(half * K, (half + 1) * K)
        return (lambda hh: k_ref[0, hh, rows, :]), (lambda hh: v_ref[0, hh, rows, :])

    def far_block(kb, carry):
        bb = pl.program_id(0)
        k0 = pl.multiple_of(kb * K, K)
        copies = [pltpu.make_async_copy(k_hbm.at[bb, :, pl.ds(k0, K), :], kfar_ref, sem_far.at[0]),
                  pltpu.make_async_copy(v_hbm.at[bb, :, pl.ds(k0, K), :], vfar_ref, sem_far.at[1])]
        for c in copies:
            c.start()
        for c in copies:
            c.wait()
        run_blocks([((lambda hh: kfar_ref[hh]), (lambda hh: vfar_ref[hh]), 0, None)])
        return carry

    run_blocks([tile_half(kprev_ref, vprev_ref, KB - 1) + (0, 'if_not_first_tile')]
               + [tile_half(kcur_ref, vcur_ref, d) + (d * K, 'diagonal') for d in range(KB)], start=True)
    lax.fori_loop(first_ref[pl.program_id(0), qi], KB * qi - 1, far_block, 0)

    low = lax.broadcasted_iota(jnp.int32, (T, PAIR_W), 1) < HEAD_DIM
    for pair in range(N_PAIRS):
        acc_a = acc_ref[2 * pair]
        acc_b = acc_ref[2 * pair + 1]
        y = jnp.where(low, acc_a / acc_a[:, HEAD_DIM:HEAD_DIM + 1], acc_b / acc_b[:, 0:1])
        yd_ref[:, pair * PAIR_W:(pair + 1) * PAIR_W] = y.astype(BF16)
    o_ref[0] = (h_ref[0]
                + jnp.dot(yc_ref[0], wout_ref[0:W, :], preferred_element_type=F32)
                + jnp.dot(yd_ref[...], wout_ref[W:2 * W, :], preferred_element_type=F32))


def _attention(q, k, v, stats, yc, h, layer, w_out):
    B, S, D = h.shape
    T, K, W = AT_T, AT_K, MIX_W
    assert OD_T == AT_T and KEY_BLOCKS_PER_TILE == 2, "tile summaries hold c at the middle and end of an odd_in tile"
    q_tile = pl.BlockSpec((1, N_HEADS, T, PAIR_W), lambda b, t: (b, 0, t, 0))
    prev_tile = pl.BlockSpec((1, N_HEADS, T, PAIR_W), lambda b, t: (b, 0, jnp.maximum(t - 1, 0), 0))
    hbm = pl.BlockSpec(memory_space=pl.ANY)
    tile = pl.BlockSpec((1, T, D), lambda b, t: (b, t, 0))
    return pl.pallas_call(
        _attn_kernel,
        out_shape=jax.ShapeDtypeStruct((B, S, D), F32),
        grid=(B, S // T),
        in_specs=[pl.BlockSpec(memory_space=pltpu.SMEM), q_tile, prev_tile, q_tile, prev_tile, q_tile, hbm, hbm,
                  pl.BlockSpec((1, T, W), lambda b, t: (b, t, 0)), tile, _layer_block((2 * W, D), layer)],
        out_specs=tile,
        scratch_shapes=[pltpu.VMEM((N_HEADS, T, K), F32),
                        pltpu.VMEM((N_HEADS, T, PAIR_W), F32),
                        pltpu.VMEM((N_HEADS, T, PAIR_W), F32),
                        pltpu.VMEM((N_HEADS, T, PAIR_W), F32),
                        pltpu.VMEM((T, W), BF16),
                        pltpu.VMEM((N_HEADS, K, PAIR_W), BF16),
                        pltpu.VMEM((N_HEADS, K, PAIR_W), BF16),
                        pltpu.SemaphoreType.DMA((2,))],
        compiler_params=pltpu.CompilerParams(dimension_semantics=("arbitrary", "arbitrary"),
                                             vmem_limit_bytes=VMEM_LIMIT_BYTES),
        name="fox_attention",
    )(_first_needed_block(stats), q, k, k, v, v, k, v, yc, h, w_out)


def kernel(x, mix0_norm, mix0_w_in, lru_conv_w, lru_conv_b, lru_wa, lru_ba, lru_wx, lru_bx, lru_lambda,
           sconv_w, sconv_b, mix0_w_out, mix1_norm, mix1_w_in, sgu_norm, sgu_w, sgu_b, fox_bf, mix1_w_out,
           ffn_norm, ffn_up, ffn_conv_w, ffn_conv_b, ffn_down, final_norm):
    h = _even_mixer(x, 0, mix0_norm, mix0_w_in, lru_conv_w, lru_conv_b, lru_wa, lru_ba, lru_wx, lru_bx,
                    lru_lambda, sconv_w, sconv_b, mix0_w_out)
    h = _ffn(h, 0, ffn_norm, ffn_up, ffn_conv_w, ffn_conv_b, ffn_down, final_norm, final_norm=False)
    yc, q, k, v, stats = _odd_in(h, 0, mix1_norm, mix1_w_in, sgu_norm, sgu_w, sgu_b, fox_bf)
    h = _attention(q, k, v, stats, yc, h, 0, mix1_w_out.astype(BF16))
    return _ffn(h, 1, ffn_norm, ffn_up, ffn_conv_w, ffn_conv_b, ffn_down, final_norm, final_norm=True)
```

```python
import functools

import jax
import jax.numpy as jnp
from jax import lax
from jax.experimental import pallas as pl
from jax.experimental.pallas import tpu as pltpu

F32 = jnp.float32
BF16 = jnp.bfloat16

EPS = 1e-6
D_MODEL = 1024
D_FF = 2816
FFN_TAPS = 3
HALO = 8
VMEM_LIMIT_BYTES = 56 * 1024 * 1024


def _rmsnorm(x, g):
    ms = jnp.mean(x * x, axis=-1, keepdims=True)
    return x * lax.rsqrt(ms + EPS) * g


def _sigmoid(x):
    return 1.0 / (1.0 + jnp.exp(-x))


def _resident(shape):
    return pl.BlockSpec(shape, lambda *_: (0,) * len(shape), pipeline_mode=pl.Buffered(1))


def _layer_block(shape, layer):
    return pl.BlockSpec((None,) + tuple(shape), lambda *_: (layer,) + (0,) * len(shape),
                        pipeline_mode=pl.Buffered(1))


def _layer_row(p, layer):
    if p.shape[0] == 1:
        return _resident(p.shape), p
    return _layer_block((1, p.shape[1]), layer), p[:, None, :]


def _stream_to_bf16(srcs, dsts, stage_ref, sems):
    copies = [pltpu.make_async_copy(src, stage_ref.at[i % 2], sems.at[i % 2]) for i, src in enumerate(srcs)]
    copies[0].start()
    for i, dst in enumerate(dsts):
        if i + 1 < len(copies):
            copies[i + 1].start()
        copies[i].wait()
        dst[...] = stage_ref[i % 2].astype(BF16)


FFN_T = 512
FFN_CH = 256
FFN_NCH = D_FF // FFN_CH
SUBLANES = 8
FFN_PAD = (FFN_TAPS - 1) * SUBLANES


def _ffn_kernel(h_hbm, g_ref, wup_hbm, cw_ref, cb_ref, wdn_hbm, fg_ref, o_hbm,
                hbuf_ref, obuf_ref, xn_ref, work0_ref, work1_ref, act0_ref, act1_ref, halo_ref, acc_ref,
                wup_ref, wdn_ref, stage_up_ref, stage_dn_ref, sem_in, sem_out, sem_w,
                *, layer, final_norm, n_tiles, n_steps):
    T, CH, PAD = FFN_T, FFN_CH, FFN_PAD
    G = T // SUBLANES
    b = pl.program_id(0)
    t = pl.program_id(1)
    n = b * n_tiles + t
    slot = lax.rem(n, 2)

    @pl.when(n == 0)
    def _():
        UC = stage_up_ref.shape[2]
        DR = stage_dn_ref.shape[1]
        _stream_to_bf16([wup_hbm.at[layer, :, pl.ds(c * UC, UC)] for c in range(wup_ref.shape[1] // UC)],
                        [wup_ref.at[:, pl.ds(c * UC, UC)] for c in range(wup_ref.shape[1] // UC)],
                        stage_up_ref, sem_w.at[0])
        _stream_to_bf16([wdn_hbm.at[layer, pl.ds(c * DR, DR), :] for c in range(wdn_ref.shape[0] // DR)],
                        [wdn_ref.at[pl.ds(c * DR, DR), :] for c in range(wdn_ref.shape[0] // DR)],
                        stage_dn_ref, sem_w.at[1])

    def in_copies(bb, tt, sl):
        return [pltpu.make_async_copy(h_hbm.at[bb, pl.ds(tt * T + r * G, G), :], hbuf_ref.at[sl, :, r, :],
                                      sem_in.at[sl]) for r in range(SUBLANES)]

    def out_copies(bb, tt, sl):
        return [pltpu.make_async_copy(obuf_ref.at[sl, :, r, :], o_hbm.at[bb, pl.ds(tt * T + r * G, G), :],
                                      sem_out.at[sl]) for r in range(SUBLANES)]

    @pl.when(n == 0)
    def _():
        for c in in_copies(b, t, slot):
            c.start()

    @pl.when(n + 1 < n_steps)
    def _():
        n1 = n + 1
        for c in in_copies(n1 // n_tiles, lax.rem(n1, n_tiles), 1 - slot):
            c.start()

    @pl.when(t == 0)
    def _():
        halo_ref[...] = jnp.zeros(halo_ref.shape, F32)

    for c in in_copies(b, t, slot):
        c.wait()
    h = hbuf_ref[slot].reshape(T, D_MODEL)
    xn_ref[...] = _rmsnorm(h, g_ref[layer:layer + 1, :]).astype(BF16)
    acc_ref[...] = h

    works = (work0_ref, work1_ref)
    acts = (act0_ref, act1_ref)
    first_row = lax.broadcasted_iota(jnp.int32, (SUBLANES, CH), 0) == 0

    def chunk_cols(j):
        return slice(j * CH, (j + 1) * CH), slice(D_FF + j * CH, D_FF + (j + 1) * CH)

    def up_proj_half(j, half):
        work_ref = works[j % 2]
        hcols = slice(half * CH, (half + 1) * CH)
        work_ref[PAD:PAD + T, hcols] = jnp.dot(xn_ref[...], wup_ref[:, chunk_cols(j)[half]],
                                               preferred_element_type=F32)
        for k in range(FFN_TAPS - 1):
            rows = slice(k * SUBLANES, (k + 1) * SUBLANES)
            tail = pltpu.roll(work_ref[T + k * SUBLANES:T + (k + 1) * SUBLANES, hcols], 1, axis=0)
            work_ref[rows, hcols] = jnp.where(first_row, halo_ref[j, rows, hcols], tail)
            halo_ref[j, rows, hcols] = tail

    def down_proj_half(j, half):
        cols = slice(half * (D_MODEL // 2), (half + 1) * (D_MODEL // 2))
        acc_ref[:, cols] += jnp.dot(acts[j % 2][...], wdn_ref[j * CH:(j + 1) * CH, cols],
                                    preferred_element_type=F32)

    up_proj_half(0, 0)
    up_proj_half(0, 1)
    for j in range(FFN_NCH):
        for half in range(2):
            if j + 1 < FFN_NCH:
                up_proj_half(j + 1, half)
            if j >= 1:
                down_proj_half(j - 1, half)
        work_ref = works[j % 2]
        gate_cols, val_cols = chunk_cols(j)
        cw = jnp.concatenate([cw_ref[:, gate_cols], cw_ref[:, val_cols]], axis=1)
        y = jnp.concatenate([cb_ref[layer:layer + 1, gate_cols], cb_ref[layer:layer + 1, val_cols]], axis=1)
        for k in range(FFN_TAPS):
            y = y + work_ref[k * SUBLANES:k * SUBLANES + T, :] * cw[k:k + 1, :]
        gate = y[:, :CH]
        val = y[:, CH:]
        acts[j % 2][...] = (gate * _sigmoid(gate) * val).astype(BF16)
    down_proj_half(FFN_NCH - 1, 0)
    down_proj_half(FFN_NCH - 1, 1)

    out = acc_ref[...]
    if final_norm:
        out = _rmsnorm(out, fg_ref[...])

    @pl.when(n >= 2)
    def _():
        for c in out_copies(b, t, slot):
            c.wait()

    obuf_ref[slot] = out.reshape(G, SUBLANES, D_MODEL)
    for c in out_copies(b, t, slot):
        c.start()

    @pl.when(n == n_steps - 1)
    def _():
        for c in out_copies(b, t, slot):
            c.wait()
        if n_steps >= 2:
            for c in out_copies(b, t, 1 - slot):
                c.wait()


def _ffn(h, layer, g, w_up, conv_w, conv_b, w_down, final_g, *, final_norm):
    B, S, D = h.shape
    T, CH, NCH = FFN_T, FFN_CH, FFN_NCH
    G = T // SUBLANES
    n_tiles = S // T
    return pl.pallas_call(
        functools.partial(_ffn_kernel, layer=layer, final_norm=final_norm, n_tiles=n_tiles, n_steps=B * n_tiles),
        out_shape=jax.ShapeDtypeStruct((B, S, D), F32),
        grid=(B, n_tiles),
        in_specs=[pl.BlockSpec(memory_space=pl.ANY), _resident(g.shape),
                  pl.BlockSpec(memory_space=pl.ANY), _layer_block((FFN_TAPS, 2 * D_FF), layer),
                  _resident(conv_b.shape), pl.BlockSpec(memory_space=pl.ANY), _resident((1, D))],
        out_specs=pl.BlockSpec(memory_space=pl.ANY),
        scratch_shapes=[pltpu.VMEM((2, G, SUBLANES, D), F32),
                        pltpu.VMEM((2, G, SUBLANES, D), F32),
                        pltpu.VMEM((T, D), BF16),
                        pltpu.VMEM((T + FFN_PAD, 2 * CH), F32),
                        pltpu.VMEM((T + FFN_PAD, 2 * CH), F32),
                        pltpu.VMEM((T, CH), BF16),
                        pltpu.VMEM((T, CH), BF16),
                        pltpu.VMEM((NCH, FFN_PAD, 2 * CH), F32),
                        pltpu.VMEM((T, D), F32),
                        pltpu.VMEM((D, 2 * D_FF), BF16),
                        pltpu.VMEM((D_FF, D), BF16),
                        pltpu.VMEM((2, D, 2 * CH), F32),
                        pltpu.VMEM((2, CH, D), F32),
                        pltpu.SemaphoreType.DMA((2,)),
                        pltpu.SemaphoreType.DMA((2,)),
                        pltpu.SemaphoreType.DMA((2, 2))],
        compiler_params=pltpu.CompilerParams(dimension_semantics=("arbitrary", "arbitrary"),
                                             vmem_limit_bytes=VMEM_LIMIT_BYTES),
        name="ffn_final" if final_norm else "ffn",
    )(h, g, w_up, conv_w, conv_b, w_down, final_g.reshape(1, D))


EV_T = 512
MIX_W = 512
LRU_TAPS = 4
SC_TAPS = 3
LRU_DECAY_C = 8.0
GELU_C = 0.7978845608028654


def _gelu(x):
    return x * (0.5 * (1.0 + jnp.tanh(GELU_C * (x + 0.044715 * (x * x * x)))))


def _causal_taps(buf_ref, w_ref, b_ref, taps, T):
    w = w_ref[...]
    y = b_ref[...]
    for k in range(taps):
        off = HALO - (taps - 1) + k
        y = y + buf_ref[off:off + T, :] * w[k:k + 1, :]
    return y


def _even_kernel(h_ref, g_ref, win_hbm, lcw_ref, lcb_ref, wg_ref, bg_ref, lam_ref, scw_ref, scb_ref, wout_hbm,
                 o_ref, xn_ref, xa_ref, cv_ref, a_ref, hs_ref, y_ref, state_ref,
                 win_ref, wout_ref, stage_in_ref, stage_out_ref, sem_w, *, layer):
    T, W = EV_T, MIX_W

    @pl.when((pl.program_id(0) == 0) & (pl.program_id(1) == 0))
    def _():
        _stream_to_bf16([win_hbm.at[layer, :, pl.ds(c * W, W)] for c in range(win_ref.shape[1] // W)],
                        [win_ref.at[:, pl.ds(c * W, W)] for c in range(win_ref.shape[1] // W)],
                        stage_in_ref, sem_w.at[0])
        _stream_to_bf16([wout_hbm.at[layer, pl.ds(c * W, W), :] for c in range(wout_ref.shape[0] // W)],
                        [wout_ref.at[pl.ds(c * W, W), :] for c in range(wout_ref.shape[0] // W)],
                        stage_out_ref, sem_w.at[1])

    @pl.when(pl.program_id(1) == 0)
    def _():
        xa_ref[0:HALO, :] = jnp.zeros((HALO, W), F32)
        cv_ref[0:HALO, :] = jnp.zeros((HALO, W), F32)
        state_ref[...] = jnp.zeros(state_ref.shape, F32)

    h = h_ref[0]
    xn_ref[...] = _rmsnorm(h, g_ref[...]).astype(BF16)

    def proj(c):
        return jnp.dot(xn_ref[...], win_ref[:, c * W:(c + 1) * W], preferred_element_type=F32)

    xa_ref[HALO:HALO + T, :] = proj(0)
    xc = _causal_taps(xa_ref, lcw_ref, lcb_ref, LRU_TAPS, T)
    xa_ref[0:HALO, :] = xa_ref[T:T + HALO, :]
    xcb = xc.astype(BF16)
    HW = W // 2
    pre = [jnp.dot(xcb[:, s * HW:(s + 1) * HW], wg_ref[s], preferred_element_type=F32) for s in range(2)]
    bg = bg_ref[...]
    r = _sigmoid(jnp.concatenate([pre[0][:, :HW], pre[1][:, :HW]], axis=1) + bg[:, :W])
    i = _sigmoid(jnp.concatenate([pre[0][:, HW:], pre[1][:, HW:]], axis=1) + bg[:, W:])
    lam = lam_ref[...]
    softplus_neg_lam = jnp.maximum(-lam, 0.0) + jnp.log1p(jnp.exp(-jnp.abs(lam)))
    log_a = (-LRU_DECAY_C * r) * softplus_neg_lam
    a = jnp.exp(log_a)
    th = jnp.tanh(log_a)
    u = jnp.sqrt((-2.0 * th) / (1.0 - th)) * (i * xc)

    G = T // 8
    a3 = a.reshape(G, 8, W)
    u3 = u.reshape(G, 8, W)
    row = lax.broadcasted_iota(jnp.int32, (G, 8, W), 1)
    for d in (1, 2, 4):
        keep = row >= d
        u3 = u3 + a3 * jnp.where(keep, pltpu.roll(u3, d, axis=1), 0.0)
        a3 = a3 * jnp.where(keep, pltpu.roll(a3, d, axis=1), 1.0)
    a_ref[...] = a3.reshape(T, W)
    hs_ref[...] = u3.reshape(T, W)
    carry = state_ref[0:1, :]
    for gi in range(G):
        rows = slice(gi * 8, gi * 8 + 8)
        hg = hs_ref[rows, :] + a_ref[rows, :] * carry
        hs_ref[rows, :] = hg
        carry = hg[7:8, :]
    state_ref[0:1, :] = carry
    y_ref[:, 0:W] = (hs_ref[...] * _gelu(proj(1))).astype(BF16)

    cv_ref[HALO:HALO + T, :] = proj(2) * proj(4)
    cc = _causal_taps(cv_ref, scw_ref, scb_ref, SC_TAPS, T)
    cv_ref[0:HALO, :] = cv_ref[T:T + HALO, :]
    y_ref[:, W:2 * W] = (proj(3) * cc).astype(BF16)

    o_ref[0] = h + jnp.dot(y_ref[...], wout_ref[...], preferred_element_type=F32)


def _block_diag(w):
    H, d, _ = w.shape
    return jnp.einsum('hij,hg->higj', w, jnp.eye(H, dtype=w.dtype)).reshape(H * d, H * d)


def _even_mixer(h, layer, g, w_in, lcw, lcb, wa, ba, wx, bx, lam, scw, scb, w_out):
    B, S, D = h.shape
    T, W = EV_T, MIX_W
    nh = wa.shape[1] // 2
    wg = jnp.stack([jnp.concatenate([_block_diag(wa[layer, s * nh:(s + 1) * nh]),
                                     _block_diag(wx[layer, s * nh:(s + 1) * nh])], axis=1) for s in range(2)])
    wg = wg.astype(BF16)
    bg = jnp.concatenate([ba[layer], bx[layer]]).reshape(1, 2 * W)
    tile = pl.BlockSpec((1, T, D), lambda b, t: (b, t, 0))
    (g_spec, g), (lcb_spec, lcb), (lam_spec, lam), (scb_spec, scb) = (_layer_row(p, layer) for p in (g, lcb, lam, scb))
    hbm = pl.BlockSpec(memory_space=pl.ANY)
    return pl.pallas_call(
        functools.partial(_even_kernel, layer=layer),
        out_shape=jax.ShapeDtypeStruct((B, S, D), F32),
        grid=(B, S // T),
        in_specs=[tile, g_spec, hbm, _layer_block((LRU_TAPS, W), layer), lcb_spec,
                  _resident((2, W // 2, W)), _resident((1, 2 * W)), lam_spec,
                  _layer_block((SC_TAPS, W), layer), scb_spec, hbm],
        out_specs=tile,
        scratch_shapes=[pltpu.VMEM((T, D), BF16),
                        pltpu.VMEM((T + HALO, W), F32),
                        pltpu.VMEM((T + HALO, W), F32),
                        pltpu.VMEM((T, W), F32),
                        pltpu.VMEM((T, W), F32),
                        pltpu.VMEM((T, 2 * W), BF16),
                        pltpu.VMEM((8, W), F32),
                        pltpu.VMEM((D, 5 * W), BF16),
                        pltpu.VMEM((2 * W, D), BF16),
                        pltpu.VMEM((2, D, W), F32),
                        pltpu.VMEM((2, W, D), F32),
                        pltpu.SemaphoreType.DMA((2, 2))],
        compiler_params=pltpu.CompilerParams(dimension_semantics=("arbitrary", "arbitrary"),
                                             vmem_limit_bytes=VMEM_LIMIT_BYTES),
        name="even_mixer",
    )(h, g, w_in, lcw, lcb, wg, bg, lam, scw, scb, w_out)


OD_T = 512
SGU_CHUNK = 128
N_HEADS = 8
HEAD_DIM = 64
F_PAD = 128
HEADS_PER_HALF = 4
QK_SCALE = HEAD_DIM ** -0.5
LOG2E = 1.4426950408889634
PAIR_W = 2 * HEAD_DIM
N_PAIRS = N_HEADS // 2


def _odd_in_kernel(h_ref, g_ref, win_hbm, wf_ref, gn_ref, seg_ref, sw_ref, sb_ref, bf_ref,
                   yc_ref, q_ref, k_ref, v_ref, st_ref, xn_ref, gv_ref, u_ref, csum_ref,
                   win_ref, stage_ref, sem_w, *, layer):
    T, W = OD_T, MIX_W

    @pl.when((pl.program_id(0) == 0) & (pl.program_id(1) == 0))
    def _():
        _stream_to_bf16([win_hbm.at[layer, :, pl.ds(c * W, W)] for c in range(win_ref.shape[1] // W)],
                        [win_ref.at[:, pl.ds(c * W, W)] for c in range(win_ref.shape[1] // W)],
                        stage_ref, sem_w)

    @pl.when(pl.program_id(1) == 0)
    def _():
        csum_ref[...] = jnp.zeros(csum_ref.shape, F32)

    xn_ref[...] = _rmsnorm(h_ref[0], g_ref[...]).astype(BF16)

    def proj(lo, width):
        return jnp.dot(xn_ref[...], win_ref[:, lo:lo + width], preferred_element_type=F32)

    fl = jnp.dot(xn_ref[...], wf_ref[...], preferred_element_type=F32) + bf_ref[...]
    logf = jnp.minimum(fl, 0.0) - jnp.log1p(jnp.exp(-jnp.abs(fl)))
    tri = (lax.broadcasted_iota(jnp.int32, (T, T), 1) <= lax.broadcasted_iota(jnp.int32, (T, T), 0)).astype(BF16)
    logf_hi = logf.astype(BF16)
    logf_mid = (logf - logf_hi.astype(F32)).astype(BF16)
    c_parts = jnp.dot(tri, jnp.concatenate([logf_hi, logf_mid], axis=1), preferred_element_type=F32)
    c_tile = c_parts[:, :F_PAD] + c_parts[:, F_PAD:] + csum_ref[0:1, :]
    csum_ref[0:1, :] = c_tile[T - 1:T, :]

    lane = lax.broadcasted_iota(jnp.int32, (T, PAIR_W), 1)
    low = lane < HEAD_DIM
    pq = proj(2 * W, W) * (QK_SCALE * LOG2E)
    pk = proj(3 * W, W)
    pv = proj(4 * W, W)
    for hh in range(N_HEADS):
        blk = slice((hh // 2) * PAIR_W, (hh // 2 + 1) * PAIR_W)
        own = low if hh % 2 == 0 else jnp.logical_not(low)
        a0 = HEAD_DIM if hh % 2 == 0 else 0
        x = jnp.broadcast_to(c_tile[:, hh:hh + 1] * (-LOG2E), (T, PAIR_W))
        hi = x.astype(BF16).astype(F32)
        r1 = x - hi
        mid = r1.astype(BF16).astype(F32)
        lo = r1 - mid
        c_parts = jnp.where(lane == a0, hi, jnp.where(lane == a0 + 1, mid, jnp.where(lane == a0 + 2, lo, 0.0)))
        ones3 = jnp.where((lane >= a0) & (lane < a0 + 3), 1.0, 0.0)
        one1 = jnp.where(lane == a0, 1.0, 0.0)
        q_ref[0, hh] = jnp.where(own, pq[:, blk], ones3).astype(BF16)
        k_ref[0, hh] = jnp.where(own, pk[:, blk], c_parts).astype(BF16)
        v_ref[0, hh] = jnp.where(own, pv[:, blk], one1).astype(BF16)

    def max_block_norm(x):
        sq = [jnp.max(jnp.sum(x[:, p * PAIR_W:(p + 1) * PAIR_W] ** 2, axis=-1, keepdims=True), axis=0, keepdims=True)
              for p in range(N_PAIRS)]
        return jnp.broadcast_to(jnp.sqrt(functools.reduce(jnp.maximum, sq)), (1, PAIR_W))

    st_ref[0, 0] = jnp.concatenate([c_tile[0:1, :], c_tile[T - 1:T, :], max_block_norm(pq), max_block_norm(pk),
                                    c_tile[T // 2 - 1:T // 2, :], jnp.zeros((3, PAIR_W), F32)], axis=0)

    u_ref[...] = _gelu(proj(0, W))
    g = _gelu(proj(W, W))
    ms = jnp.dot((g * g).astype(BF16), seg_ref[...], preferred_element_type=F32) * (1.0 / HEAD_DIM)
    gv_ref[...] = (g * lax.rsqrt(ms + EPS) * gn_ref[...]).astype(BF16)

    HW = HEADS_PER_HALF * HEAD_DIM
    lane_head = lax.broadcasted_iota(jnp.int32, (SGU_CHUNK, HW), 1) // HEAD_DIM
    wrow = lax.broadcasted_iota(jnp.int32, (SGU_CHUNK, HEADS_PER_HALF * SGU_CHUNK), 0)
    wcol = lax.broadcasted_iota(jnp.int32, (SGU_CHUNK, HEADS_PER_HALF * SGU_CHUNK), 1) % SGU_CHUNK
    for half in range(2):
        cols = slice(half * HW, (half + 1) * HW)
        w_causal = jnp.where(wcol <= wrow, sw_ref[half], 0.0).astype(BF16)
        bias = sb_ref[:, cols]
        for c in range(T // SGU_CHUNK):
            rows = slice(c * SGU_CHUNK, (c + 1) * SGU_CHUNK)
            gvc = gv_ref[rows, cols]
            stacked = jnp.concatenate(
                [jnp.where(lane_head == j, gvc, jnp.zeros_like(gvc)) for j in range(HEADS_PER_HALF)], axis=0)
            mixed = jnp.dot(w_causal, stacked, preferred_element_type=F32) + bias
            yc_ref[0, rows, cols] = (u_ref[rows, cols] * mixed).astype(BF16)


def _odd_in(h, layer, g, w_in, sgu_norm, sgu_w, sgu_b, fox_bf):
    B, S, D = h.shape
    T, W = OD_T, MIX_W
    wf = jnp.pad(w_in[layer, :, 5 * W:], ((0, 0), (0, F_PAD - (w_in.shape[2] - 5 * W)))).astype(BF16)
    seg =_block_diag(jnp.ones((N_HEADS, HEAD_DIM, HEAD_DIM), F32)).astype(BF16)
    sw = sgu_w[layer].reshape(2, HEADS_PER_HALF, SGU_CHUNK, SGU_CHUNK).transpose(0, 2, 1, 3).reshape(
        2, SGU_CHUNK, HEADS_PER_HALF * SGU_CHUNK)
    sb = jnp.repeat(sgu_b[layer].T, HEAD_DIM, axis=1)
    bf = jnp.pad(fox_bf[layer], (0, F_PAD - N_HEADS)).reshape(1, F_PAD)
    (g_spec, g), (gn_spec, sgu_norm) = _layer_row(g, layer), _layer_row(sgu_norm, layer)
    tile = pl.BlockSpec((1, T, D), lambda b, t: (b, t, 0))
    half_tile = pl.BlockSpec((1, T, W), lambda b, t: (b, t, 0))
    head_tile = pl.BlockSpec((1, N_HEADS, T, PAIR_W), lambda b, t: (b, 0, t, 0))
    act = jax.ShapeDtypeStruct((B, S, W), BF16)
    per_head = jax.ShapeDtypeStruct((B, N_HEADS, S, PAIR_W), BF16)
    return pl.pallas_call(
        functools.partial(_odd_in_kernel, layer=layer),
        out_shape=(act, per_head, per_head, per_head, jax.ShapeDtypeStruct((B, S // T, 8, PAIR_W), F32)),
        grid=(B, S // T),
        in_specs=[tile, g_spec, pl.BlockSpec(memory_space=pl.ANY), _resident((D, F_PAD)), gn_spec,
                  _resident((W, W)), _resident((2, SGU_CHUNK, HEADS_PER_HALF * SGU_CHUNK)),
                  _resident((SGU_CHUNK, W)), _resident((1, F_PAD))],
        out_specs=(half_tile, head_tile, head_tile, head_tile,
                   pl.BlockSpec((1, 1, 8, PAIR_W), lambda b, t: (b, t, 0, 0))),
        scratch_shapes=[pltpu.VMEM((T, D), BF16),
                        pltpu.VMEM((T, W), BF16),
                        pltpu.VMEM((T, W), F32),
                        pltpu.VMEM((N_HEADS, 128), F32),
                        pltpu.VMEM((D, 5 * W), BF16),
                        pltpu.VMEM((2, D, W), F32),
                        pltpu.SemaphoreType.DMA((2,))],
        compiler_params=pltpu.CompilerParams(dimension_semantics=("arbitrary", "arbitrary"),
                                             vmem_limit_bytes=VMEM_LIMIT_BYTES),
        name="odd_in",
    )(h, g, w_in, wf, sgu_norm, seg, sw, sb, bf)


AT_T = 512
AT_K = 256
KEY_BLOCKS_PER_TILE = AT_T // AT_K
M_INIT = -1e30
QK_AHEAD = 2
SKIP_LOG2 = -160.0
NORM_SLACK = 1.05


def _first_needed_block(stats):
    KB = KEY_BLOCKS_PER_TILE
    batch, n = stats.shape[:2]
    c_first = stats[:, :, 0, :N_HEADS]
    c_end = jnp.stack([stats[:, :, 4, :N_HEADS], stats[:, :, 1, :N_HEADS]], axis=2).reshape(batch, KB * n, N_HEADS)
    q_norm = stats[:, :, 2, 0]
    k_norm = jnp.max(stats[:, :, 3, 0], axis=1, keepdims=True)
    spread = NORM_SLACK * 2.0 * q_norm * k_norm
    decay = (c_end[:, None, :, :] - c_first[:, :, None, :]) * LOG2E
    block = lax.broadcasted_iota(jnp.int32, (n, KB * n), 1)
    tile = lax.broadcasted_iota(jnp.int32, (n, KB * n), 0)
    worst = jnp.max(spread[:, :, None, None] - decay, axis=-1)
    needed = (worst >= SKIP_LOG2) | (block >= KB * tile)[None]
    return jnp.min(jnp.where(needed, block[None], KB * n), axis=-1).astype(jnp.int32)


def _attn_kernel(first_ref, q_ref, kprev_ref, kcur_ref, vprev_ref, vcur_ref, k_hbm, v_hbm, yc_ref, h_ref, wout_ref,
                 o_ref, s_ref, mx_ref, m_ref, acc_ref, yd_ref, kfar_ref, vfar_ref, sem_far):
    T, K, W, KB = AT_T, AT_K, MIX_W, KEY_BLOCKS_PER_TILE
    qi = pl.program_id(1)

    def run_blocks(blocks):
        assert all(blk[2] == 0 for blk in blocks if blk[4] in ('init', 'flip'))
        items = [(blk, hh) for blk in blocks for hh in range(N_HEADS)]

        def logits(item):
            (keys, _, r0, mask, mode), hh = item
            rows = slice(r0, T)
            s = lax.dot_general(q_ref[0, hh, rows, :], keys(hh), (((1,), (1,)), ((), ())),
                                preferred_element_type=F32)
            if mask == 'diagonal':
                visible = (lax.broadcasted_iota(jnp.int32, (T - r0, K), 1)
                           <= lax.broadcasted_iota(jnp.int32, (T - r0, K), 0))
                s = jnp.where(visible, s, -jnp.inf)
            elif mask == 'if_not_first_tile':
                s = jnp.where(qi > 0, s, -jnp.inf)
            s_ref[hh, rows, :] = s
            row_max = jnp.broadcast_to(jnp.max(s, axis=-1, keepdims=True), (T - r0, PAIR_W))
            if mode == 'init':
                mx_ref[hh, rows, :] = jnp.maximum(row_max, M_INIT)
            elif mode == 'flip':
                m_ref[hh, rows, :] = jnp.maximum(mx_ref[hh, rows, :], row_max)
            else:
                mx_ref[hh, rows, :] = jnp.maximum(m_ref[hh, rows, :], row_max)

        def update(item):
            (_, values, r0, _, mode), hh = item
            rows = slice(r0, T)
            old_ref, new_ref = (mx_ref, m_ref) if mode == 'flip' else (m_ref, mx_ref)
            m_new = new_ref[hh, rows, :]
            p = jnp.exp2(s_ref[hh, rows, :] - jnp.concatenate([m_new] * (K // PAIR_W), axis=1))
            pv = jnp.dot(p.astype(BF16), values(hh), preferred_element_type=F32)
            if mode == 'init':
                acc_ref[hh, rows, :] = pv
            else:
                acc_ref[hh, rows, :] = jnp.exp2(old_ref[hh, rows, :] - m_new) * acc_ref[hh, rows, :] + pv
            if mode == 'keep':
                m_ref[hh, rows, :] = m_new

        for item in items[:QK_AHEAD]:
            logits(item)
        for i, item in enumerate(items):
            if i + QK_AHEAD < len(items):
                logits(items[i + QK_AHEAD])
            update(item)

    def tile_half(k_ref, v_ref, half):
        rows = slice(half * K, (half + 1) * K)
        return (lambda hh: k_ref[0, hh, rows, :]), (lambda hh: v_ref[0, hh, rows, :])

    def far_block(kb, carry):
        bb = pl.program_id(0)
        k0 = pl.multiple_of(kb * K, K)
        copies = [pltpu.make_async_copy(k_hbm.at[bb, :, pl.ds(k0, K), :], kfar_ref, sem_far.at[0]),
                  pltpu.make_async_copy(v_hbm.at[bb, :, pl.ds(k0, K), :], vfar_ref, sem_far.at[1])]
        for c in copies:
            c.start()
        for c in copies:
            c.wait()
        run_blocks([((lambda hh: kfar_ref[hh]), (lambda hh: vfar_ref[hh]), 0, None, 'keep')])
        return carry

    run_blocks([tile_half(kprev_ref, vprev_ref, KB - 1) + (0, 'if_not_first_tile', 'init')]
               + [tile_half(kcur_ref, vcur_ref, d) + (d * K, 'diagonal', 'flip' if d == 0 else 'keep')
                  for d in range(KB)])
    lax.fori_loop(first_ref[pl.program_id(0), qi], KB * qi - 1, far_block, 0)

    low = lax.broadcasted_iota(jnp.int32, (T, PAIR_W), 1) < HEAD_DIM
    for pair in range(N_PAIRS):
        acc_a = acc_ref[2 * pair]
        acc_b = acc_ref[2 * pair + 1]
        y = jnp.where(low, acc_a / acc_a[:, HEAD_DIM:HEAD_DIM + 1], acc_b / acc_b[:, 0:1])
        yd_ref[:, pair * PAIR_W:(pair + 1) * PAIR_W] = y.astype(BF16)
    o_ref[0] = (h_ref[0]
                + jnp.dot(yc_ref[0], wout_ref[0:W, :], preferred_element_type=F32)
                + jnp.dot(yd_ref[...], wout_ref[W:2 * W, :], preferred_element_type=F32))


def _attention(q, k, v, stats, yc, h, layer, w_out):
    B, S, D = h.shape
    T, K, W = AT_T, AT_K, MIX_W
    assert OD_T == AT_T and KEY_BLOCKS_PER_TILE == 2, "tile summaries hold c at the middle and end of an odd_in tile"
    q_tile = pl.BlockSpec((1, N_HEADS, T, PAIR_W), lambda b, t: (b, 0, t, 0))
    prev_tile = pl.BlockSpec((1, N_HEADS, T, PAIR_W), lambda b, t: (b, 0, jnp.maximum(t - 1, 0), 0))
    hbm = pl.BlockSpec(memory_space=pl.ANY)
    tile = pl.BlockSpec((1, T, D), lambda b, t: (b, t, 0))
    return pl.pallas_call(
        _attn_kernel,
        out_shape=jax.ShapeDtypeStruct((B, S, D), F32),
        grid=(B, S // T),
        in_specs=[pl.BlockSpec(memory_space=pltpu.SMEM), q_tile, prev_tile, q_tile, prev_tile, q_tile, hbm, hbm,
                  pl.BlockSpec((1, T, W), lambda b, t: (b, t, 0)), tile, _layer_block((2 * W, D), layer)],
        out_specs=tile,
        scratch_shapes=[pltpu.VMEM((N_HEADS, T, K), F32),
                        pltpu.VMEM((N_HEADS, T, PAIR_W), F32),
                        pltpu.VMEM((N_HEADS, T, PAIR_W), F32),
                        pltpu.VMEM((N_HEADS, T, PAIR_W), F32),
                        pltpu.VMEM((T, W), BF16),
                        pltpu.VMEM((N_HEADS, K, PAIR_W), BF16),
                        pltpu.VMEM((N_HEADS, K, PAIR_W), BF16),
                        pltpu.SemaphoreType.DMA((2,))],
        compiler_params=pltpu.CompilerParams(dimension_semantics=("arbitrary", "arbitrary"),
                                             vmem_limit_bytes=VMEM_LIMIT_BYTES),
        name="fox_attention",
    )(_first_needed_block(stats), q, k, k, v, v, k, v, yc, h, w_out)


def kernel(x, mix0_norm, mix0_w_in, lru_conv_w, lru_conv_b, lru_wa, lru_ba, lru_wx, lru_bx, lru_lambda,
           sconv_w, sconv_b, mix0_w_out, mix1_norm, mix1_w_in, sgu_norm, sgu_w, sgu_b, fox_bf, mix1_w_out,
           ffn_norm, ffn_up, ffn_conv_w, ffn_conv_b, ffn_down, final_norm):
    h = _even_mixer(x, 0, mix0_norm, mix0_w_in, lru_conv_w, lru_conv_b, lru_wa, lru_ba, lru_wx, lru_bx,
                    lru_lambda, sconv_w, sconv_b, mix0_w_out)
    h = _ffn(h, 0, ffn_norm, ffn_up, ffn_conv_w, ffn_conv_b, ffn_down, final_norm, final_norm=False)
    yc, q, k, v, stats = _odd_in(h, 0, mix1_norm, mix1_w_in, sgu_norm, sgu_w, sgu_b, fox_bf)
    h = _attention(q, k, v, stats, yc, h, 0, mix1_w_out.astype(BF16))
    return _ffn(h, 1, ffn_norm, ffn_up, ffn_conv_w, ffn_conv_b, ffn_down, final_norm, final_norm=True)
```

```python
import functools

import jax
import jax.numpy as jnp
from jax import lax
from jax.experimental import pallas as pl
from jax.experimental.pallas import tpu as pltpu

F32 = jnp.float32
BF16 = jnp.bfloat16

EPS = 1e-6
D_MODEL = 1024
D_FF = 2816
FFN_TAPS = 3
HALO = 8
VMEM_LIMIT_BYTES = 56 * 1024 * 1024


def _rmsnorm(x, g):
    ms = jnp.mean(x * x, axis=-1, keepdims=True)
    return x * lax.rsqrt(ms + EPS) * g


def _sigmoid(x):
    return 1.0 / (1.0 + jnp.exp(-x))


def _resident(shape):
    return pl.BlockSpec(shape, lambda *_: (0,) * len(shape), pipeline_mode=pl.Buffered(1))


def _layer_block(shape, layer):
    return pl.BlockSpec((None,) + tuple(shape), lambda *_: (layer,) + (0,) * len(shape),
                        pipeline_mode=pl.Buffered(1))


def _layer_row(p, layer):
    if p.shape[0] == 1:
        return _resident(p.shape), p
    return _layer_block((1, p.shape[1]), layer), p[:, None, :]


def _stream_to_bf16(srcs, dsts, stage_ref, sems):
    copies = [pltpu.make_async_copy(src, stage_ref.at[i % 2], sems.at[i % 2]) for i, src in enumerate(srcs)]
    copies[0].start()
    for i, dst in enumerate(dsts):
        if i + 1 < len(copies):
            copies[i + 1].start()
        copies[i].wait()
        dst[...] = stage_ref[i % 2].astype(BF16)


FFN_T = 512
FFN_CH = 256
FFN_NCH = D_FF // FFN_CH
SUBLANES = 8
FFN_PAD = (FFN_TAPS - 1) * SUBLANES
N_IN_SLOTS = 3


def _ffn_kernel(h_hbm, g_ref, wup_hbm, cw_ref, cb_ref, wdn_hbm, fg_ref, o_hbm,
                hbuf_ref, obuf_ref, xn_ref, work0_ref, work1_ref, act0_ref, act1_ref, halo_ref, acc_ref,
                wup_ref, wdn_ref, stage_up_ref, stage_dn_ref, sem_in, sem_out, sem_w,
                *, layer, final_norm, n_tiles, n_steps):
    T, CH, PAD = FFN_T, FFN_CH, FFN_PAD
    G = T // SUBLANES
    b = pl.program_id(0)
    t = pl.program_id(1)
    n = b * n_tiles + t
    slot = lax.rem(n, 2)
    in_slot = lax.rem(n, N_IN_SLOTS)
    next_in_slot = lax.rem(n + 1, N_IN_SLOTS)

    @pl.when(n == 0)
    def _():
        UC = stage_up_ref.shape[2]
        DR = stage_dn_ref.shape[1]
        _stream_to_bf16([wup_hbm.at[layer, :, pl.ds(c * UC, UC)] for c in range(wup_ref.shape[1] // UC)],
                        [wup_ref.at[:, pl.ds(c * UC, UC)] for c in range(wup_ref.shape[1] // UC)],
                        stage_up_ref, sem_w.at[0])
        _stream_to_bf16([wdn_hbm.at[layer, pl.ds(c * DR, DR), :] for c in range(wdn_ref.shape[0] // DR)],
                        [wdn_ref.at[pl.ds(c * DR, DR), :] for c in range(wdn_ref.shape[0] // DR)],
                        stage_dn_ref, sem_w.at[1])

    def in_copies(bb, tt, sl):
        return [pltpu.make_async_copy(h_hbm.at[bb, pl.ds(tt * T + r * G, G), :], hbuf_ref.at[sl, :, r, :],
                                      sem_in.at[sl]) for r in range(SUBLANES)]

    def out_copies(bb, tt, sl):
        return [pltpu.make_async_copy(obuf_ref.at[sl, :, r, :], o_hbm.at[bb, pl.ds(tt * T + r * G, G), :],
                                      sem_out.at[sl]) for r in range(SUBLANES)]

    def normalised(sl):
        return _rmsnorm(hbuf_ref[sl].reshape(T, D_MODEL), g_ref[layer:layer + 1, :]).astype(BF16)

    @pl.when(n == 0)
    def _():
        for c in in_copies(b, t, in_slot):
            c.start()
        if n_steps > 1:
            for c in in_copies(1 // n_tiles, 1 % n_tiles, 1 % N_IN_SLOTS):
                c.start()
        for c in in_copies(b, t, in_slot):
            c.wait()
        xn_ref[0] = normalised(0)

    @pl.when(n + 2 < n_steps)
    def _():
        n2 = n + 2
        for c in in_copies(n2 // n_tiles, lax.rem(n2, n_tiles), lax.rem(n2, N_IN_SLOTS)):
            c.start()

    @pl.when(n + 1 < n_steps)
    def _():
        for c in in_copies(b, t, next_in_slot):
            c.wait()

    @pl.when(t == 0)
    def _():
        halo_ref[...] = jnp.zeros(halo_ref.shape, F32)

    works = (work0_ref, work1_ref)
    acts = (act0_ref, act1_ref)
    first_row = lax.broadcasted_iota(jnp.int32, (SUBLANES, CH), 0) == 0

    def chunk_cols(j):
        return slice(j * CH, (j + 1) * CH), slice(D_FF + j * CH, D_FF + (j + 1) * CH)

    def up_proj_half(j, half):
        work_ref = works[j % 2]
        hcols = slice(half * CH, (half + 1) * CH)
        work_ref[PAD:PAD + T, hcols] = jnp.dot(xn_ref[slot], wup_ref[:, chunk_cols(j)[half]],
                                               preferred_element_type=F32)
        for k in range(FFN_TAPS - 1):
            rows = slice(k * SUBLANES, (k + 1) * SUBLANES)
            tail = pltpu.roll(work_ref[T + k * SUBLANES:T + (k + 1) * SUBLANES, hcols], 1, axis=0)
            work_ref[rows, hcols] = jnp.where(first_row, halo_ref[j, rows, hcols], tail)
            halo_ref[j, rows, hcols] = tail

    def down_proj_half(j, half):
        cols = slice(half * (D_MODEL // 2), (half + 1) * (D_MODEL // 2))
        part = jnp.dot(acts[j % 2][...], wdn_ref[j * CH:(j + 1) * CH, cols], preferred_element_type=F32)
        if j == 0:
            acc_ref[:, cols] = part
        else:
            acc_ref[:, cols] += part

    up_proj_half(0, 0)
    up_proj_half(0, 1)
    for j in range(FFN_NCH):
        for half in range(2):
            if j + 1 < FFN_NCH:
                up_proj_half(j + 1, half)
            if j >= 1:
                down_proj_half(j - 1, half)
        work_ref = works[j % 2]
        gate_cols, val_cols = chunk_cols(j)
        cw = jnp.concatenate([cw_ref[:, gate_cols], cw_ref[:, val_cols]], axis=1)
        y = jnp.concatenate([cb_ref[layer:layer + 1, gate_cols], cb_ref[layer:layer + 1, val_cols]], axis=1)
        for k in range(FFN_TAPS):
            y = y + work_ref[k * SUBLANES:k * SUBLANES + T, :] * cw[k:k + 1, :]
        gate = y[:, :CH]
        val = y[:, CH:]
        acts[j % 2][...] = (gate * _sigmoid(gate) * val).astype(BF16)
    down_proj_half(FFN_NCH - 1, 0)
    down_proj_half(FFN_NCH - 1, 1)

    xn_ref[1 - slot] = normalised(next_in_slot)

    out = hbuf_ref[in_slot].reshape(T, D_MODEL) + acc_ref[...]
    if final_norm:
        out = _rmsnorm(out, fg_ref[...])

    @pl.when(n >= 2)
    def _():
        for c in out_copies(b, t, slot):
            c.wait()

    obuf_ref[slot] = out.reshape(G, SUBLANES, D_MODEL)
    for c in out_copies(b, t, slot):
        c.start()

    @pl.when(n == n_steps - 1)
    def _():
        for c in out_copies(b, t, slot):
            c.wait()
        if n_steps >= 2:
            for c in out_copies(b, t, 1 - slot):
                c.wait()


def _ffn(h, layer, g, w_up, conv_w, conv_b, w_down, final_g, *, final_norm):
    B, S, D = h.shape
    T, CH, NCH = FFN_T, FFN_CH, FFN_NCH
    G = T // SUBLANES
    n_tiles = S // T
    return pl.pallas_call(
        functools.partial(_ffn_kernel, layer=layer, final_norm=final_norm, n_tiles=n_tiles, n_steps=B * n_tiles),
        out_shape=jax.ShapeDtypeStruct((B, S, D), F32),
        grid=(B, n_tiles),
        in_specs=[pl.BlockSpec(memory_space=pl.ANY), _resident(g.shape),
                  pl.BlockSpec(memory_space=pl.ANY), _layer_block((FFN_TAPS, 2 * D_FF), layer),
                  _resident(conv_b.shape), pl.BlockSpec(memory_space=pl.ANY), _resident((1, D))],
        out_specs=pl.BlockSpec(memory_space=pl.ANY),
        scratch_shapes=[pltpu.VMEM((N_IN_SLOTS, G, SUBLANES, D), F32),
                        pltpu.VMEM((2, G, SUBLANES, D), F32),
                        pltpu.VMEM((2, T, D), BF16),
                        pltpu.VMEM((T + FFN_PAD, 2 * CH), F32),
                        pltpu.VMEM((T + FFN_PAD, 2 * CH), F32),
                        pltpu.VMEM((T, CH), BF16),
                        pltpu.VMEM((T, CH), BF16),
                        pltpu.VMEM((NCH, FFN_PAD, 2 * CH), F32),
                        pltpu.VMEM((T, D), F32),
                        pltpu.VMEM((D, 2 * D_FF), BF16),
                        pltpu.VMEM((D_FF, D), BF16),
                        pltpu.VMEM((2, D, 2 * CH), F32),
                        pltpu.VMEM((2, CH, D), F32),
                        pltpu.SemaphoreType.DMA((N_IN_SLOTS,)),
                        pltpu.SemaphoreType.DMA((2,)),
                        pltpu.SemaphoreType.DMA((2, 2))],
        compiler_params=pltpu.CompilerParams(dimension_semantics=("arbitrary", "arbitrary"),
                                             vmem_limit_bytes=VMEM_LIMIT_BYTES),
        name="ffn_final" if final_norm else "ffn",
    )(h, g, w_up, conv_w, conv_b, w_down, final_g.reshape(1, D))


EV_T = 512
MIX_W = 512
LRU_TAPS = 4
SC_TAPS = 3
LRU_DECAY_C = 8.0
GELU_C = 0.7978845608028654


def _gelu(x):
    return x * (0.5 * (1.0 + jnp.tanh(GELU_C * (x + 0.044715 * (x * x * x)))))


def _causal_taps(buf_ref, w_ref, b_ref, taps, T):
    w = w_ref[...]
    y = b_ref[...]
    for k in range(taps):
        off = HALO - (taps - 1) + k
        y = y + buf_ref[off:off + T, :] * w[k:k + 1, :]
    return y


def _even_kernel(h_ref, g_ref, win_hbm, lcw_ref, lcb_ref, wg_ref, bg_ref, lam_ref, scw_ref, scb_ref, wout_hbm,
                 o_ref, xn_ref, xa_ref, cv_ref, a_ref, hs_ref, y_ref, state_ref,
                 win_ref, wout_ref, stage_in_ref, stage_out_ref, sem_w, *, layer):
    T, W = EV_T, MIX_W

    @pl.when((pl.program_id(0) == 0) & (pl.program_id(1) == 0))
    def _():
        _stream_to_bf16([win_hbm.at[layer, :, pl.ds(c * W, W)] for c in range(win_ref.shape[1] // W)],
                        [win_ref.at[:, pl.ds(c * W, W)] for c in range(win_ref.shape[1] // W)],
                        stage_in_ref, sem_w.at[0])
        _stream_to_bf16([wout_hbm.at[layer, pl.ds(c * W, W), :] for c in range(wout_ref.shape[0] // W)],
                        [wout_ref.at[pl.ds(c * W, W), :] for c in range(wout_ref.shape[0] // W)],
                        stage_out_ref, sem_w.at[1])

    @pl.when(pl.program_id(1) == 0)
    def _():
        xa_ref[0:HALO, :] = jnp.zeros((HALO, W), F32)
        cv_ref[0:HALO, :] = jnp.zeros((HALO, W), F32)
        state_ref[...] = jnp.zeros(state_ref.shape, F32)

    h = h_ref[0]
    xn_ref[...] = _rmsnorm(h, g_ref[...]).astype(BF16)

    def proj(c):
        return jnp.dot(xn_ref[...], win_ref[:, c * W:(c + 1) * W], preferred_element_type=F32)

    xa_ref[HALO:HALO + T, :] = proj(0)
    xc = _causal_taps(xa_ref, lcw_ref, lcb_ref, LRU_TAPS, T)
    xa_ref[0:HALO, :] = xa_ref[T:T + HALO, :]
    xcb = xc.astype(BF16)
    HW = W // 2
    pre = [jnp.dot(xcb[:, s * HW:(s + 1) * HW], wg_ref[s], preferred_element_type=F32) for s in range(2)]
    bg = bg_ref[...]
    r = _sigmoid(jnp.concatenate([pre[0][:, :HW], pre[1][:, :HW]], axis=1) + bg[:, :W])
    i = _sigmoid(jnp.concatenate([pre[0][:, HW:], pre[1][:, HW:]], axis=1) + bg[:, W:])
    lam = lam_ref[...]
    softplus_neg_lam = jnp.maximum(-lam, 0.0) + jnp.log1p(jnp.exp(-jnp.abs(lam)))
    log_a = (-LRU_DECAY_C * r) * softplus_neg_lam
    a = jnp.exp(log_a)
    th = jnp.tanh(log_a)
    u = jnp.sqrt((-2.0 * th) / (1.0 - th)) * (i * xc)

    G = T // 8
    a3 = a.reshape(G, 8, W)
    u3 = u.reshape(G, 8, W)
    row = lax.broadcasted_iota(jnp.int32, (G, 8, W), 1)
    for d in (1, 2, 4):
        keep = row >= d
        u3 = u3 + a3 * jnp.where(keep, pltpu.roll(u3, d, axis=1), 0.0)
        a3 = a3 * jnp.where(keep, pltpu.roll(a3, d, axis=1), 1.0)
    a_ref[...] = a3.reshape(T, W)
    hs_ref[...] = u3.reshape(T, W)
    carry = state_ref[0:1, :]
    for gi in range(G):
        rows = slice(gi * 8, gi * 8 + 8)
        hg = hs_ref[rows, :] + a_ref[rows, :] * carry
        hs_ref[rows, :] = hg
        carry = hg[7:8, :]
    state_ref[0:1, :] = carry
    y_ref[:, 0:W] = (hs_ref[...] * _gelu(proj(1))).astype(BF16)

    cv_ref[HALO:HALO + T, :] = proj(2) * proj(4)
    cc = _causal_taps(cv_ref, scw_ref, scb_ref, SC_TAPS, T)
    cv_ref[0:HALO, :] = cv_ref[T:T + HALO, :]
    y_ref[:, W:2 * W] = (proj(3) * cc).astype(BF16)

    o_ref[0] = h + jnp.dot(y_ref[...], wout_ref[...], preferred_element_type=F32)


def _block_diag(w):
    H, d, _ = w.shape
    return jnp.einsum('hij,hg->higj', w, jnp.eye(H, dtype=w.dtype)).reshape(H * d, H * d)


def _even_mixer(h, layer, g, w_in, lcw, lcb, wa, ba, wx, bx, lam, scw, scb, w_out):
    B, S, D = h.shape
    T, W = EV_T, MIX_W
    nh = wa.shape[1] // 2
    wg = jnp.stack([jnp.concatenate([_block_diag(wa[layer, s * nh:(s + 1) * nh]),
                                     _block_diag(wx[layer, s * nh:(s + 1) * nh])], axis=1) for s in range(2)])
    wg = wg.astype(BF16)
    bg = jnp.concatenate([ba[layer], bx[layer]]).reshape(1, 2 * W)
    tile = pl.BlockSpec((1, T, D), lambda b, t: (b, t, 0))
    (g_spec, g), (lcb_spec, lcb), (lam_spec, lam), (scb_spec, scb) = (_layer_row(p, layer) for p in (g, lcb, lam, scb))
    hbm = pl.BlockSpec(memory_space=pl.ANY)
    return pl.pallas_call(
        functools.partial(_even_kernel, layer=layer),
        out_shape=jax.ShapeDtypeStruct((B, S, D), F32),
        grid=(B, S // T),
        in_specs=[tile, g_spec, hbm, _layer_block((LRU_TAPS, W), layer), lcb_spec,
                  _resident((2, W // 2, W)), _resident((1, 2 * W)), lam_spec,
                  _layer_block((SC_TAPS, W), layer), scb_spec, hbm],
        out_specs=tile,
        scratch_shapes=[pltpu.VMEM((T, D), BF16),
                        pltpu.VMEM((T + HALO, W), F32),
                        pltpu.VMEM((T + HALO, W), F32),
                        pltpu.VMEM((T, W), F32),
                        pltpu.VMEM((T, W), F32),
                        pltpu.VMEM((T, 2 * W), BF16),
                        pltpu.VMEM((8, W), F32),
                        pltpu.VMEM((D, 5 * W), BF16),
                        pltpu.VMEM((2 * W, D), BF16),
                        pltpu.VMEM((2, D, W), F32),
                        pltpu.VMEM((2, W, D), F32),
                        pltpu.SemaphoreType.DMA((2, 2))],
        compiler_params=pltpu.CompilerParams(dimension_semantics=("arbitrary", "arbitrary"),
                                             vmem_limit_bytes=VMEM_LIMIT_BYTES),
        name="even_mixer",
    )(h, g, w_in, lcw, lcb, wg, bg, lam, scw, scb, w_out)


OD_T = 512
SGU_CHUNK = 128
N_HEADS = 8
HEAD_DIM = 64
F_PAD = 128
HEADS_PER_HALF = 4
QK_SCALE = HEAD_DIM ** -0.5
LOG2E = 1.4426950408889634
PAIR_W = 2 * HEAD_DIM
N_PAIRS = N_HEADS // 2


def _odd_in_kernel(h_ref, g_ref, win_hbm, wf_ref, gn_ref, seg_ref, sw_ref, sb_ref, bf_ref,
                   yc_ref, q_ref, k_ref, v_ref, st_ref, xn_ref, gv_ref, u_ref, csum_ref,
                   win_ref, stage_ref, sem_w, *, layer):
    T, W = OD_T, MIX_W

    @pl.when((pl.program_id(0) == 0) & (pl.program_id(1) == 0))
    def _():
        _stream_to_bf16([win_hbm.at[layer, :, pl.ds(c * W, W)] for c in range(win_ref.shape[1] // W)],
                        [win_ref.at[:, pl.ds(c * W, W)] for c in range(win_ref.shape[1] // W)],
                        stage_ref, sem_w)

    @pl.when(pl.program_id(1) == 0)
    def _():
        csum_ref[...] = jnp.zeros(csum_ref.shape, F32)

    xn_ref[...] = _rmsnorm(h_ref[0], g_ref[...]).astype(BF16)

    def proj(lo, width):
        return jnp.dot(xn_ref[...], win_ref[:, lo:lo + width], preferred_element_type=F32)

    fl = jnp.dot(xn_ref[...], wf_ref[...], preferred_element_type=F32) + bf_ref[...]
    logf = jnp.minimum(fl, 0.0) - jnp.log1p(jnp.exp(-jnp.abs(fl)))
    tri = (lax.broadcasted_iota(jnp.int32, (T, T), 1) <= lax.broadcasted_iota(jnp.int32, (T, T), 0)).astype(BF16)
    logf_hi = logf.astype(BF16)
    logf_mid = (logf - logf_hi.astype(F32)).astype(BF16)
    c_parts = jnp.dot(tri, jnp.concatenate([logf_hi, logf_mid], axis=1), preferred_element_type=F32)
    c_tile = c_parts[:, :F_PAD] + c_parts[:, F_PAD:] + csum_ref[0:1, :]
    csum_ref[0:1, :] = c_tile[T - 1:T, :]

    lane = lax.broadcasted_iota(jnp.int32, (T, PAIR_W), 1)
    low = lane < HEAD_DIM
    pq = proj(2 * W, W) * (QK_SCALE * LOG2E)
    pk = proj(3 * W, W)
    pv = proj(4 * W, W)
    for hh in range(N_HEADS):
        blk = slice((hh // 2) * PAIR_W, (hh // 2 + 1) * PAIR_W)
        own = low if hh % 2 == 0 else jnp.logical_not(low)
        a0 = HEAD_DIM if hh % 2 == 0 else 0
        x = jnp.broadcast_to(c_tile[:, hh:hh + 1] * (-LOG2E), (T, PAIR_W))
        hi = x.astype(BF16).astype(F32)
        r1 = x - hi
        mid = r1.astype(BF16).astype(F32)
        lo = r1 - mid
        c_parts = jnp.where(lane == a0, hi, jnp.where(lane == a0 + 1, mid, jnp.where(lane == a0 + 2, lo, 0.0)))
        ones3 = jnp.where((lane >= a0) & (lane < a0 + 3), 1.0, 0.0)
        one1 = jnp.where(lane == a0, 1.0, 0.0)
        q_ref[0, hh] = jnp.where(own, pq[:, blk], ones3).astype(BF16)
        k_ref[0, hh] = jnp.where(own, pk[:, blk], c_parts).astype(BF16)
        v_ref[0, hh] = jnp.where(own, pv[:, blk], one1).astype(BF16)

    def max_block_norm(x):
        sq = [jnp.max(jnp.sum(x[:, p * PAIR_W:(p + 1) * PAIR_W] ** 2, axis=-1, keepdims=True), axis=0, keepdims=True)
              for p in range(N_PAIRS)]
        return jnp.broadcast_to(jnp.sqrt(functools.reduce(jnp.maximum, sq)), (1, PAIR_W))

    st_ref[0, 0] = jnp.concatenate([c_tile[0:1, :], c_tile[T - 1:T, :], max_block_norm(pq), max_block_norm(pk),
                                    c_tile[T // 2 - 1:T // 2, :], jnp.zeros((3, PAIR_W), F32)], axis=0)

    u_ref[...] = _gelu(proj(0, W))
    g = _gelu(proj(W, W))
    ms = jnp.dot((g * g).astype(BF16), seg_ref[...], preferred_element_type=F32) * (1.0 / HEAD_DIM)
    gv_ref[...] = (g * lax.rsqrt(ms + EPS) * gn_ref[...]).astype(BF16)

    HW = HEADS_PER_HALF * HEAD_DIM
    lane_head = lax.broadcasted_iota(jnp.int32, (SGU_CHUNK, HW), 1) // HEAD_DIM
    wrow = lax.broadcasted_iota(jnp.int32, (SGU_CHUNK, HEADS_PER_HALF * SGU_CHUNK), 0)
    wcol = lax.broadcasted_iota(jnp.int32, (SGU_CHUNK, HEADS_PER_HALF * SGU_CHUNK), 1) % SGU_CHUNK
    for half in range(2):
        cols = slice(half * HW, (half + 1) * HW)
        w_causal = jnp.where(wcol <= wrow, sw_ref[half], 0.0).astype(BF16)
        bias = sb_ref[:, cols]
        for c in range(T // SGU_CHUNK):
            rows = slice(c * SGU_CHUNK, (c + 1) * SGU_CHUNK)
            gvc = gv_ref[rows, cols]
            stacked = jnp.concatenate(
                [jnp.where(lane_head == j, gvc, jnp.zeros_like(gvc)) for j in range(HEADS_PER_HALF)], axis=0)
            mixed = jnp.dot(w_causal, stacked, preferred_element_type=F32) + bias
            yc_ref[0, rows, cols] = (u_ref[rows, cols] * mixed).astype(BF16)


def _odd_in(h, layer, g, w_in, sgu_norm, sgu_w, sgu_b, fox_bf):
    B, S, D = h.shape
    T, W = OD_T, MIX_W
    wf = jnp.pad(w_in[layer, :, 5 * W:], ((0, 0), (0, F_PAD - (w_in.shape[2] - 5 * W)))).astype(BF16)
    seg =_block_diag(jnp.ones((N_HEADS, HEAD_DIM, HEAD_DIM), F32)).astype(BF16)
    sw = sgu_w[layer].reshape(2, HEADS_PER_HALF, SGU_CHUNK, SGU_CHUNK).transpose(0, 2, 1, 3).reshape(
        2, SGU_CHUNK, HEADS_PER_HALF * SGU_CHUNK)
    sb = jnp.repeat(sgu_b[layer].T, HEAD_DIM, axis=1)
    bf = jnp.pad(fox_bf[layer], (0, F_PAD - N_HEADS)).reshape(1, F_PAD)
    (g_spec, g), (gn_spec, sgu_norm) = _layer_row(g, layer), _layer_row(sgu_norm, layer)
    tile = pl.BlockSpec((1, T, D), lambda b, t: (b, t, 0))
    half_tile = pl.BlockSpec((1, T, W), lambda b, t: (b, t, 0))
    head_tile = pl.BlockSpec((1, N_HEADS, T, PAIR_W), lambda b, t: (b, 0, t, 0))
    act = jax.ShapeDtypeStruct((B, S, W), BF16)
    per_head = jax.ShapeDtypeStruct((B, N_HEADS, S, PAIR_W), BF16)
    return pl.pallas_call(
        functools.partial(_odd_in_kernel, layer=layer),
        out_shape=(act, per_head, per_head, per_head, jax.ShapeDtypeStruct((B, S // T, 8, PAIR_W), F32)),
        grid=(B, S // T),
        in_specs=[tile, g_spec, pl.BlockSpec(memory_space=pl.ANY), _resident((D, F_PAD)), gn_spec,
                  _resident((W, W)), _resident((2, SGU_CHUNK, HEADS_PER_HALF * SGU_CHUNK)),
                  _resident((SGU_CHUNK, W)), _resident((1, F_PAD))],
        out_specs=(half_tile, head_tile, head_tile, head_tile,
                   pl.BlockSpec((1, 1, 8, PAIR_W), lambda b, t: (b, t, 0, 0))),
        scratch_shapes=[pltpu.VMEM((T, D), BF16),
                        pltpu.VMEM((T, W), BF16),
                        pltpu.VMEM((T, W), F32),
                        pltpu.VMEM((N_HEADS, 128), F32),
                        pltpu.VMEM((D, 5 * W), BF16),
                        pltpu.VMEM((2, D, W), F32),
                        pltpu.SemaphoreType.DMA((2,))],
        compiler_params=pltpu.CompilerParams(dimension_semantics=("arbitrary", "arbitrary"),
                                             vmem_limit_bytes=VMEM_LIMIT_BYTES),
        name="odd_in",
    )(h, g, w_in, wf, sgu_norm, seg, sw, sb, bf)


AT_T = 512
AT_K = 256
KEY_BLOCKS_PER_TILE = AT_T // AT_K
M_INIT = -1e30
QK_AHEAD = 2
SKIP_LOG2 = -160.0
NORM_SLACK = 1.05


def _first_needed_block(stats):
    KB = KEY_BLOCKS_PER_TILE
    batch, n = stats.shape[:2]
    c_first = stats[:, :, 0, :N_HEADS]
    c_end = jnp.stack([stats[:, :, 4, :N_HEADS], stats[:, :, 1, :N_HEADS]], axis=2).reshape(batch, KB * n, N_HEADS)
    q_norm = stats[:, :, 2, 0]
    k_norm = jnp.max(stats[:, :, 3, 0], axis=1, keepdims=True)
    spread = NORM_SLACK * 2.0 * q_norm * k_norm
    decay = (c_end[:, None, :, :] - c_first[:, :, None, :]) * LOG2E
    block = lax.broadcasted_iota(jnp.int32, (n, KB * n), 1)
    tile = lax.broadcasted_iota(jnp.int32, (n, KB * n), 0)
    worst = jnp.max(spread[:, :, None, None] - decay, axis=-1)
    needed = (worst >= SKIP_LOG2) | (block >= KB * tile)[None]
    return jnp.min(jnp.where(needed, block[None], KB * n), axis=-1).astype(jnp.int32)


def _attn_kernel(first_ref, q_ref, kprev_ref, kcur_ref, vprev_ref, vcur_ref, k_hbm, v_hbm, yc_ref, h_ref, wout_ref,
                 o_ref, s_ref, mx_ref, m_ref, acc_ref, yd_ref, kfar_ref, vfar_ref, sem_far):
    T, K, W, KB = AT_T, AT_K, MIX_W, KEY_BLOCKS_PER_TILE
    qi = pl.program_id(1)

    def run_blocks(blocks):
        assert all(blk[2] == 0 for blk in blocks if blk[4] in ('init', 'flip'))
        items = [(blk, hh) for blk in blocks for hh in range(N_HEADS)]

        def logits(item):
            (keys, _, r0, mask, mode), hh = item
            rows = slice(r0, T)
            s = lax.dot_general(q_ref[0, hh, rows, :], keys(hh), (((1,), (1,)), ((), ())),
                                preferred_element_type=F32)
            if mask == 'diagonal':
                visible = (lax.broadcasted_iota(jnp.int32, (T - r0, K), 1)
                           <= lax.broadcasted_iota(jnp.int32, (T - r0, K), 0))
                s = jnp.where(visible, s, -jnp.inf)
            elif mask == 'if_not_first_tile':
                s = jnp.where(qi > 0, s, -jnp.inf)
            s_ref[hh, rows, :] = s
            row_max = jnp.broadcast_to(jnp.max(s, axis=-1, keepdims=True), (T - r0, PAIR_W))
            if mode == 'init':
                mx_ref[hh, rows, :] = jnp.maximum(row_max, M_INIT)
            elif mode == 'flip':
                m_ref[hh, rows, :] = jnp.maximum(mx_ref[hh, rows, :], row_max)
            else:
                mx_ref[hh, rows, :] = jnp.maximum(m_ref[hh, rows, :], row_max)

        def update(item):
            (_, values, r0, _, mode), hh = item
            rows = slice(r0, T)
            old_ref, new_ref = (mx_ref, m_ref) if mode == 'flip' else (m_ref, mx_ref)
            m_new = new_ref[hh, rows, :]
            p = jnp.exp2(s_ref[hh, rows, :] - jnp.concatenate([m_new] * (K // PAIR_W), axis=1))
            pv = jnp.dot(p.astype(BF16), values(hh), preferred_element_type=F32)
            if mode == 'init':
                acc_ref[hh, rows, :] = pv
            else:
                acc_ref[hh, rows, :] = jnp.exp2(old_ref[hh, rows, :] - m_new) * acc_ref[hh, rows, :] + pv
            if mode == 'keep':
                m_ref[hh, rows, :] = m_new

        for item in items[:QK_AHEAD]:
            logits(item)
        for i, item in enumerate(items):
            if i + QK_AHEAD < len(items):
                logits(items[i + QK_AHEAD])
            update(item)

    def tile_half(k_ref, v_ref, half):
        rows = slice(half * K, (half + 1) * K)
        return (lambda hh: k_ref[0, hh, rows, :]), (lambda hh: v_ref[0, hh, rows, :])

    def far_block(kb, carry):
        bb = pl.program_id(0)
        k0 = pl.multiple_of(kb * K, K)
        copies = [pltpu.make_async_copy(k_hbm.at[bb, :, pl.ds(k0, K), :], kfar_ref, sem_far.at[0]),
                  pltpu.make_async_copy(v_hbm.at[bb, :, pl.ds(k0, K), :], vfar_ref, sem_far.at[1])]
        for c in copies:
            c.start()
        for c in copies:
            c.wait()
        run_blocks([((lambda hh: kfar_ref[hh]), (lambda hh: vfar_ref[hh]), 0, None, 'keep')])
        return carry

    run_blocks([tile_half(kprev_ref, vprev_ref, KB - 1) + (0, 'if_not_first_tile', 'init')]
               + [tile_half(kcur_ref, vcur_ref, d) + (d * K, 'diagonal', 'flip' if d == 0 else 'keep')
                  for d in range(KB)])
    lax.fori_loop(first_ref[pl.program_id(0), qi], KB * qi - 1, far_block, 0)

    low = lax.broadcasted_iota(jnp.int32, (T, PAIR_W), 1) < HEAD_DIM
    for pair in range(N_PAIRS):
        acc_a = acc_ref[2 * pair]
        acc_b = acc_ref[2 * pair + 1]
        y = jnp.where(low, acc_a / acc_a[:, HEAD_DIM:HEAD_DIM + 1], acc_b / acc_b[:, 0:1])
        yd_ref[:, pair * PAIR_W:(pair + 1) * PAIR_W] = y.astype(BF16)
    o_ref[0] = (h_ref[0]
                + jnp.dot(yc_ref[0], wout_ref[0:W, :], preferred_element_type=F32)
                + jnp.dot(yd_ref[...], wout_ref[W:2 * W, :], preferred_element_type=F32))


def _attention(q, k, v, stats, yc, h, layer, w_out):
    B, S, D = h.shape
    T, K, W = AT_T, AT_K, MIX_W
    assert OD_T == AT_T and KEY_BLOCKS_PER_TILE == 2, "tile summaries hold c at the middle and end of an odd_in tile"
    q_tile = pl.BlockSpec((1, N_HEADS, T, PAIR_W), lambda b, t: (b, 0, t, 0))
    prev_tile = pl.BlockSpec((1, N_HEADS, T, PAIR_W), lambda b, t: (b, 0, jnp.maximum(t - 1, 0), 0))
    hbm = pl.BlockSpec(memory_space=pl.ANY)
    tile = pl.BlockSpec((1, T, D), lambda b, t: (b, t, 0))
    return pl.pallas_call(
        _attn_kernel,
        out_shape=jax.ShapeDtypeStruct((B, S, D), F32),
        grid=(B, S // T),
        in_specs=[pl.BlockSpec(memory_space=pltpu.SMEM), q_tile, prev_tile, q_tile, prev_tile, q_tile, hbm, hbm,
                  pl.BlockSpec((1, T, W), lambda b, t: (b, t, 0)), tile, _layer_block((2 * W, D), layer)],
        out_specs=tile,
        scratch_shapes=[pltpu.VMEM((N_HEADS, T, K), F32),
                        pltpu.VMEM((N_HEADS, T, PAIR_W), F32),
                        pltpu.VMEM((N_HEADS, T, PAIR_W), F32),
                        pltpu.VMEM((N_HEADS, T, PAIR_W), F32),
                        pltpu.VMEM((T, W), BF16),
                        pltpu.VMEM((N_HEADS, K, PAIR_W), BF16),
                        pltpu.VMEM((N_HEADS, K, PAIR_W), BF16),
                        pltpu.SemaphoreType.DMA((2,))],
        compiler_params=pltpu.CompilerParams(dimension_semantics=("arbitrary", "arbitrary"),
                                             vmem_limit_bytes=VMEM_LIMIT_BYTES),
        name="fox_attention",
    )(_first_needed_block(stats), q, k, k, v, v, k, v, yc, h, w_out)


def kernel(x, mix0_norm, mix0_w_in, lru_conv_w, lru_conv_b, lru_wa, lru_ba, lru_wx, lru_bx, lru_lambda,
           sconv_w, sconv_b, mix0_w_out, mix1_norm, mix1_w_in, sgu_norm, sgu_w, sgu_b, fox_bf, mix1_w_out,
           ffn_norm, ffn_up, ffn_conv_w, ffn_conv_b, ffn_down, final_norm):
    h = _even_mixer(x, 0, mix0_norm, mix0_w_in, lru_conv_w, lru_conv_b, lru_wa, lru_ba, lru_wx, lru_bx,
                    lru_lambda, sconv_w, sconv_b, mix0_w_out)
    h = _ffn(h, 0, ffn_norm, ffn_up, ffn_conv_w, ffn_conv_b, ffn_down, final_norm, final_norm=False)
    yc, q, k, v, stats = _odd_in(h, 0, mix1_norm, mix1_w_in, sgu_norm, sgu_w, sgu_b, fox_bf)
    h = _attention(q, k, v, stats, yc, h, 0, mix1_w_out.astype(BF16))
    return _ffn(h, 1, ffn_norm, ffn_up, ffn_conv_w, ffn_conv_b, ffn_down, final_norm, final_norm=True)
```

```python
import functools

import jax
import jax.numpy as jnp
from jax import lax
from jax.experimental import pallas as pl
from jax.experimental.pallas import tpu as pltpu

F32 = jnp.float32
BF16 = jnp.bfloat16

EPS = 1e-6
D_MODEL = 1024
D_FF = 2816
FFN_TAPS = 3
HALO = 8
VMEM_LIMIT_BYTES = 56 * 1024 * 1024


def _rmsnorm(x, g):
    ms = jnp.mean(x * x, axis=-1, keepdims=True)
    return x * lax.rsqrt(ms + EPS) * g


def _sigmoid(x):
    return 1.0 / (1.0 + jnp.exp(-x))


def _resident(shape):
    return pl.BlockSpec(shape, lambda *_: (0,) * len(shape), pipeline_mode=pl.Buffered(1))


def _layer_block(shape, layer):
    return pl.BlockSpec((None,) + tuple(shape), lambda *_: (layer,) + (0,) * len(shape),
                        pipeline_mode=pl.Buffered(1))


def _layer_row(p, layer):
    if p.shape[0] == 1:
        return _resident(p.shape), p
    return _layer_block((1, p.shape[1]), layer), p[:, None, :]


def _stream_to_bf16(srcs, dsts, stage_ref, sems):
    copies = [pltpu.make_async_copy(src, stage_ref.at[i % 2], sems.at[i % 2]) for i, src in enumerate(srcs)]
    copies[0].start()
    for i, dst in enumerate(dsts):
        if i + 1 < len(copies):
            copies[i + 1].start()
        copies[i].wait()
        dst[...] = stage_ref[i % 2].astype(BF16)


FFN_T = 512
FFN_CH = 256
FFN_NCH = D_FF // FFN_CH
SUBLANES = 8
FFN_PAD = (FFN_TAPS - 1) * SUBLANES


def _ffn_kernel(h_hbm, g_ref, wup_hbm, cw_ref, cb_ref, wdn_hbm, fg_ref, o_hbm,
                hbuf_ref, obuf_ref, xn_ref, work0_ref, work1_ref, act0_ref, act1_ref, halo_ref, acc_ref,
                wup_ref, wdn_ref, stage_up_ref, stage_dn_ref, sem_in, sem_out, sem_w,
                *, layer, final_norm, n_tiles, n_steps):
    T, CH, PAD = FFN_T, FFN_CH, FFN_PAD
    G = T // SUBLANES
    b = pl.program_id(0)
    t = pl.program_id(1)
    n = b * n_tiles + t
    slot = lax.rem(n, 2)

    @pl.when(n == 0)
    def _():
        UC = stage_up_ref.shape[2]
        DR = stage_dn_ref.shape[1]
        _stream_to_bf16([wup_hbm.at[layer, :, pl.ds(c * UC, UC)] for c in range(wup_ref.shape[1] // UC)],
                        [wup_ref.at[:, pl.ds(c * UC, UC)] for c in range(wup_ref.shape[1] // UC)],
                        stage_up_ref, sem_w.at[0])
        _stream_to_bf16([wdn_hbm.at[layer, pl.ds(c * DR, DR), :] for c in range(wdn_ref.shape[0] // DR)],
                        [wdn_ref.at[pl.ds(c * DR, DR), :] for c in range(wdn_ref.shape[0] // DR)],
                        stage_dn_ref, sem_w.at[1])

    def in_copies(bb, tt, sl):
        return [pltpu.make_async_copy(h_hbm.at[bb, pl.ds(tt * T + r * G, G), :], hbuf_ref.at[sl, :, r, :],
                                      sem_in.at[sl]) for r in range(SUBLANES)]

    def out_copies(bb, tt, sl):
        return [pltpu.make_async_copy(obuf_ref.at[sl, :, r, :], o_hbm.at[bb, pl.ds(tt * T + r * G, G), :],
                                      sem_out.at[sl]) for r in range(SUBLANES)]

    @pl.when(n == 0)
    def _():
        for c in in_copies(b, t, slot):
            c.start()

    @pl.when(n + 1 < n_steps)
    def _():
        n1 = n + 1
        for c in in_copies(n1 // n_tiles, lax.rem(n1, n_tiles), 1 - slot):
            c.start()

    @pl.when(t == 0)
    def _():
        halo_ref[...] = jnp.zeros(halo_ref.shape, F32)

    for c in in_copies(b, t, slot):
        c.wait()
    h = hbuf_ref[slot].reshape(T, D_MODEL)
    xn_ref[...] = _rmsnorm(h, g_ref[layer:layer + 1, :]).astype(BF16)
    acc_ref[...] = h

    works = (work0_ref, work1_ref)
    acts = (act0_ref, act1_ref)
    first_row = lax.broadcasted_iota(jnp.int32, (SUBLANES, CH), 0) == 0

    def chunk_cols(j):
        return slice(j * CH, (j + 1) * CH), slice(D_FF + j * CH, D_FF + (j + 1) * CH)

    def up_proj_half(j, half):
        work_ref = works[j % 2]
        hcols = slice(half * CH, (half + 1) * CH)
        work_ref[PAD:PAD + T, hcols] = jnp.dot(xn_ref[...], wup_ref[:, chunk_cols(j)[half]],
                                               preferred_element_type=F32)
        for k in range(FFN_TAPS - 1):
            rows = slice(k * SUBLANES, (k + 1) * SUBLANES)
            tail = pltpu.roll(work_ref[T + k * SUBLANES:T + (k + 1) * SUBLANES, hcols], 1, axis=0)
            work_ref[rows, hcols] = jnp.where(first_row, halo_ref[j, rows, hcols], tail)
            halo_ref[j, rows, hcols] = tail

    def down_proj_half(j, half):
        cols = slice(half * (D_MODEL // 2), (half + 1) * (D_MODEL // 2))
        acc_ref[:, cols] += jnp.dot(acts[j % 2][...], wdn_ref[j * CH:(j + 1) * CH, cols],
                                    preferred_element_type=F32)

    up_proj_half(0, 0)
    up_proj_half(0, 1)
    for j in range(FFN_NCH):
        for half in range(2):
            if j + 1 < FFN_NCH:
                up_proj_half(j + 1, half)
            if j >= 1:
                down_proj_half(j - 1, half)
        work_ref = works[j % 2]
        gate_cols, val_cols = chunk_cols(j)
        cw = jnp.concatenate([cw_ref[:, gate_cols], cw_ref[:, val_cols]], axis=1)
        y = jnp.concatenate([cb_ref[layer:layer + 1, gate_cols], cb_ref[layer:layer + 1, val_cols]], axis=1)
        for k in range(FFN_TAPS):
            y = y + work_ref[k * SUBLANES:k * SUBLANES + T, :] * cw[k:k + 1, :]
        gate = y[:, :CH]
        val = y[:, CH:]
        acts[j % 2][...] = (gate * _sigmoid(gate) * val).astype(BF16)
    down_proj_half(FFN_NCH - 1, 0)
    down_proj_half(FFN_NCH - 1, 1)

    out = acc_ref[...]
    if final_norm:
        out = _rmsnorm(out, fg_ref[...])

    @pl.when(n >= 2)
    def _():
        for c in out_copies(b, t, slot):
            c.wait()

    obuf_ref[slot] = out.reshape(G, SUBLANES, D_MODEL)
    for c in out_copies(b, t, slot):
        c.start()

    @pl.when(n == n_steps - 1)
    def _():
        for c in out_copies(b, t, slot):
            c.wait()
        if n_steps >= 2:
            for c in out_copies(b, t, 1 - slot):
                c.wait()


def _ffn(h, layer, g, w_up, conv_w, conv_b, w_down, final_g, *, final_norm):
    B, S, D = h.shape
    T, CH, NCH = FFN_T, FFN_CH, FFN_NCH
    G = T // SUBLANES
    n_tiles = S // T
    return pl.pallas_call(
        functools.partial(_ffn_kernel, layer=layer, final_norm=final_norm, n_tiles=n_tiles, n_steps=B * n_tiles),
        out_shape=jax.ShapeDtypeStruct((B, S, D), F32),
        grid=(B, n_tiles),
        in_specs=[pl.BlockSpec(memory_space=pl.ANY), _resident(g.shape),
                  pl.BlockSpec(memory_space=pl.ANY), _layer_block((FFN_TAPS, 2 * D_FF), layer),
                  _resident(conv_b.shape), pl.BlockSpec(memory_space=pl.ANY), _resident((1, D))],
        out_specs=pl.BlockSpec(memory_space=pl.ANY),
        scratch_shapes=[pltpu.VMEM((2, G, SUBLANES, D), F32),
                        pltpu.VMEM((2, G, SUBLANES, D), F32),
                        pltpu.VMEM((T, D), BF16),
                        pltpu.VMEM((T + FFN_PAD, 2 * CH), F32),
                        pltpu.VMEM((T + FFN_PAD, 2 * CH), F32),
                        pltpu.VMEM((T, CH), BF16),
                        pltpu.VMEM((T, CH), BF16),
                        pltpu.VMEM((NCH, FFN_PAD, 2 * CH), F32),
                        pltpu.VMEM((T, D), F32),
                        pltpu.VMEM((D, 2 * D_FF), BF16),
                        pltpu.VMEM((D_FF, D), BF16),
                        pltpu.VMEM((2, D, 2 * CH), F32),
                        pltpu.VMEM((2, CH, D), F32),
                        pltpu.SemaphoreType.DMA((2,)),
                        pltpu.SemaphoreType.DMA((2,)),
                        pltpu.SemaphoreType.DMA((2, 2))],
        compiler_params=pltpu.CompilerParams(dimension_semantics=("arbitrary", "arbitrary"),
                                             vmem_limit_bytes=VMEM_LIMIT_BYTES),
        name="ffn_final" if final_norm else "ffn",
    )(h, g, w_up, conv_w, conv_b, w_down, final_g.reshape(1, D))


EV_T = 512
MIX_W = 512
LRU_TAPS = 4
SC_TAPS = 3
LRU_DECAY_C = 8.0
GELU_C = 0.7978845608028654


def _gelu(x):
    return x * (0.5 * (1.0 + jnp.tanh(GELU_C * (x + 0.044715 * (x * x * x)))))


def _causal_taps(buf_ref, w_ref, b_ref, taps, T):
    w = w_ref[...]
    y = b_ref[...]
    for k in range(taps):
        off = HALO - (taps - 1) + k
        y = y + buf_ref[off:off + T, :] * w[k:k + 1, :]
    return y


def _even_kernel(h_ref, g_ref, win_hbm, lcw_ref, lcb_ref, wg_ref, bg_ref, lam_ref, scw_ref, scb_ref, wout_hbm,
                 o_ref, xn_ref, xa_ref, cv_ref, a_ref, hs_ref, y_ref, state_ref,
                 win_ref, wout_ref, stage_in_ref, stage_out_ref, ga_ref, sem_w, *, layer):
    T, W = EV_T, MIX_W

    @pl.when((pl.program_id(0) == 0) & (pl.program_id(1) == 0))
    def _():
        _stream_to_bf16([win_hbm.at[layer, :, pl.ds(c * W, W)] for c in range(win_ref.shape[1] // W)],
                        [win_ref.at[:, pl.ds(c * W, W)] for c in range(win_ref.shape[1] // W)],
                        stage_in_ref, sem_w.at[0])
        _stream_to_bf16([wout_hbm.at[layer, pl.ds(c * W, W), :] for c in range(wout_ref.shape[0] // W)],
                        [wout_ref.at[pl.ds(c * W, W), :] for c in range(wout_ref.shape[0] // W)],
                        stage_out_ref, sem_w.at[1])

    @pl.when(pl.program_id(1) == 0)
    def _():
        xa_ref[0:HALO, :] = jnp.zeros((HALO, W), F32)
        cv_ref[0:HALO, :] = jnp.zeros((HALO, W), F32)
        state_ref[...] = jnp.zeros(state_ref.shape, F32)

    h = h_ref[0]
    xn_ref[...] = _rmsnorm(h, g_ref[...]).astype(BF16)

    def proj(c):
        return jnp.dot(xn_ref[...], win_ref[:, c * W:(c + 1) * W], preferred_element_type=F32)

    xa_ref[HALO:HALO + T, :] = proj(0)
    ga_ref[...] = proj(1)
    xc = _causal_taps(xa_ref, lcw_ref, lcb_ref, LRU_TAPS, T)
    xa_ref[0:HALO, :] = xa_ref[T:T + HALO, :]
    xcb = xc.astype(BF16)
    HW = W // 2
    pre = [jnp.dot(xcb[:, s * HW:(s + 1) * HW], wg_ref[s], preferred_element_type=F32) for s in range(2)]
    bg = bg_ref[...]
    r = _sigmoid(jnp.concatenate([pre[0][:, :HW], pre[1][:, :HW]], axis=1) + bg[:, :W])
    i = _sigmoid(jnp.concatenate([pre[0][:, HW:], pre[1][:, HW:]], axis=1) + bg[:, W:])
    lam = lam_ref[...]
    softplus_neg_lam = jnp.maximum(-lam, 0.0) + jnp.log1p(jnp.exp(-jnp.abs(lam)))
    log_a = (-LRU_DECAY_C * r) * softplus_neg_lam
    a = jnp.exp(log_a)
    th = jnp.tanh(log_a)
    u = jnp.sqrt((-2.0 * th) / (1.0 - th)) * (i * xc)

    G = T // 8
    a3 = a.reshape(G, 8, W)
    u3 = u.reshape(G, 8, W)
    row = lax.broadcasted_iota(jnp.int32, (G, 8, W), 1)
    for d in (1, 2, 4):
        keep = row >= d
        u3 = u3 + a3 * jnp.where(keep, pltpu.roll(u3, d, axis=1), 0.0)
        a3 = a3 * jnp.where(keep, pltpu.roll(a3, d, axis=1), 1.0)
    a_ref[...] = a3.reshape(T, W)
    hs_ref[...] = u3.reshape(T, W)
    carry = state_ref[0:1, :]
    for gi in range(G):
        rows = slice(gi * 8, gi * 8 + 8)
        hg = hs_ref[rows, :] + a_ref[rows, :] * carry
        hs_ref[rows, :] = hg
        carry = hg[7:8, :]
    state_ref[0:1, :] = carry
    y_ref[:, 0:W] = (hs_ref[...] * _gelu(ga_ref[...])).astype(BF16)

    cv_ref[HALO:HALO + T, :] = proj(2) * proj(4)
    cc = _causal_taps(cv_ref, scw_ref, scb_ref, SC_TAPS, T)
    cv_ref[0:HALO, :] = cv_ref[T:T + HALO, :]
    y_ref[:, W:2 * W] = (proj(3) * cc).astype(BF16)

    o_ref[0] = h + jnp.dot(y_ref[...], wout_ref[...], preferred_element_type=F32)


def _block_diag(w):
    H, d, _ = w.shape
    return jnp.einsum('hij,hg->higj', w, jnp.eye(H, dtype=w.dtype)).reshape(H * d, H * d)


def _even_mixer(h, layer, g, w_in, lcw, lcb, wa, ba, wx, bx, lam, scw, scb, w_out):
    B, S, D = h.shape
    T, W = EV_T, MIX_W
    nh = wa.shape[1] // 2
    wg = jnp.stack([jnp.concatenate([_block_diag(wa[layer, s * nh:(s + 1) * nh]),
                                     _block_diag(wx[layer, s * nh:(s + 1) * nh])], axis=1) for s in range(2)])
    wg = wg.astype(BF16)
    bg = jnp.concatenate([ba[layer], bx[layer]]).reshape(1, 2 * W)
    tile = pl.BlockSpec((1, T, D), lambda b, t: (b, t, 0))
    (g_spec, g), (lcb_spec, lcb), (lam_spec, lam), (scb_spec, scb) = (_layer_row(p, layer) for p in (g, lcb, lam, scb))
    hbm = pl.BlockSpec(memory_space=pl.ANY)
    return pl.pallas_call(
        functools.partial(_even_kernel, layer=layer),
        out_shape=jax.ShapeDtypeStruct((B, S, D), F32),
        grid=(B, S // T),
        in_specs=[tile, g_spec, hbm, _layer_block((LRU_TAPS, W), layer), lcb_spec,
                  _resident((2, W // 2, W)), _resident((1, 2 * W)), lam_spec,
                  _layer_block((SC_TAPS, W), layer), scb_spec, hbm],
        out_specs=tile,
        scratch_shapes=[pltpu.VMEM((T, D), BF16),
                        pltpu.VMEM((T + HALO, W), F32),
                        pltpu.VMEM((T + HALO, W), F32),
                        pltpu.VMEM((T, W), F32),
                        pltpu.VMEM((T, W), F32),
                        pltpu.VMEM((T, 2 * W), BF16),
                        pltpu.VMEM((8, W), F32),
                        pltpu.VMEM((D, 5 * W), BF16),
                        pltpu.VMEM((2 * W, D), BF16),
                        pltpu.VMEM((2, D, W), F32),
                        pltpu.VMEM((2, W, D), F32),
                        pltpu.VMEM((T, W), F32),
                        pltpu.SemaphoreType.DMA((2, 2))],
        compiler_params=pltpu.CompilerParams(dimension_semantics=("arbitrary", "arbitrary"),
                                             vmem_limit_bytes=VMEM_LIMIT_BYTES),
        name="even_mixer",
    )(h, g, w_in, lcw, lcb, wg, bg, lam, scw, scb, w_out)


OD_T = 512
SGU_CHUNK = 128
N_HEADS = 8
HEAD_DIM = 64
F_PAD = 128
HEADS_PER_HALF = 4
QK_SCALE = HEAD_DIM ** -0.5
LOG2E = 1.4426950408889634
PAIR_W = 2 * HEAD_DIM
N_PAIRS = N_HEADS // 2


def _odd_in_kernel(h_ref, g_ref, win_hbm, wf_ref, gn_ref, seg_ref, sw_ref, sb_ref, bf_ref,
                   yc_ref, q_ref, k_ref, v_ref, st_ref, xn_ref, gv_ref, u_ref, csum_ref,
                   win_ref, stage_ref, sem_w, *, layer):
    T, W = OD_T, MIX_W

    @pl.when((pl.program_id(0) == 0) & (pl.program_id(1) == 0))
    def _():
        _stream_to_bf16([win_hbm.at[layer, :, pl.ds(c * W, W)] for c in range(win_ref.shape[1] // W)],
                        [win_ref.at[:, pl.ds(c * W, W)] for c in range(win_ref.shape[1] // W)],
                        stage_ref, sem_w)

    @pl.when(pl.program_id(1) == 0)
    def _():
        csum_ref[...] = jnp.zeros(csum_ref.shape, F32)

    xn_ref[...] = _rmsnorm(h_ref[0], g_ref[...]).astype(BF16)

    def proj(lo, width):
        return jnp.dot(xn_ref[...], win_ref[:, lo:lo + width], preferred_element_type=F32)

    fl = jnp.dot(xn_ref[...], wf_ref[...], preferred_element_type=F32) + bf_ref[...]
    logf = jnp.minimum(fl, 0.0) - jnp.log1p(jnp.exp(-jnp.abs(fl)))
    tri = (lax.broadcasted_iota(jnp.int32, (T, T), 1) <= lax.broadcasted_iota(jnp.int32, (T, T), 0)).astype(BF16)
    logf_hi = logf.astype(BF16)
    logf_mid = (logf - logf_hi.astype(F32)).astype(BF16)
    c_parts = jnp.dot(tri, jnp.concatenate([logf_hi, logf_mid], axis=1), preferred_element_type=F32)
    c_tile = c_parts[:, :F_PAD] + c_parts[:, F_PAD:] + csum_ref[0:1, :]
    csum_ref[0:1, :] = c_tile[T - 1:T, :]

    lane = lax.broadcasted_iota(jnp.int32, (T, PAIR_W), 1)
    low = lane < HEAD_DIM
    pq = proj(2 * W, W) * (QK_SCALE * LOG2E)
    pk = proj(3 * W, W)
    pv = proj(4 * W, W)
    for hh in range(N_HEADS):
        blk = slice((hh // 2) * PAIR_W, (hh // 2 + 1) * PAIR_W)
        own = low if hh % 2 == 0 else jnp.logical_not(low)
        a0 = HEAD_DIM if hh % 2 == 0 else 0
        x = jnp.broadcast_to(c_tile[:, hh:hh + 1] * (-LOG2E), (T, PAIR_W))
        hi = x.astype(BF16).astype(F32)
        r1 = x - hi
        mid = r1.astype(BF16).astype(F32)
        lo = r1 - mid
        c_parts = jnp.where(lane == a0, hi, jnp.where(lane == a0 + 1, mid, jnp.where(lane == a0 + 2, lo, 0.0)))
        ones3 = jnp.where((lane >= a0) & (lane < a0 + 3), 1.0, 0.0)
        one1 = jnp.where(lane == a0, 1.0, 0.0)
        q_ref[0, hh] = jnp.where(own, pq[:, blk], ones3).astype(BF16)
        k_ref[0, hh] = jnp.where(own, pk[:, blk], c_parts).astype(BF16)
        v_ref[0, hh] = jnp.where(own, pv[:, blk], one1).astype(BF16)

    def max_block_norm(x):
        sq = [jnp.max(jnp.sum(x[:, p * PAIR_W:(p + 1) * PAIR_W] ** 2, axis=-1, keepdims=True), axis=0, keepdims=True)
              for p in range(N_PAIRS)]
        return jnp.broadcast_to(jnp.sqrt(functools.reduce(jnp.maximum, sq)), (1, PAIR_W))

    st_ref[0, 0] = jnp.concatenate([c_tile[0:1, :], c_tile[T - 1:T, :], max_block_norm(pq), max_block_norm(pk),
                                    c_tile[T // 2 - 1:T // 2, :], jnp.zeros((3, PAIR_W), F32)], axis=0)

    u_ref[...] = _gelu(proj(0, W))
    g = _gelu(proj(W, W))
    ms = jnp.dot((g * g).astype(BF16), seg_ref[...], preferred_element_type=F32) * (1.0 / HEAD_DIM)
    gv_ref[...] = (g * lax.rsqrt(ms + EPS) * gn_ref[...]).astype(BF16)

    HW = HEADS_PER_HALF * HEAD_DIM
    lane_head = lax.broadcasted_iota(jnp.int32, (SGU_CHUNK, HW), 1) // HEAD_DIM
    wrow = lax.broadcasted_iota(jnp.int32, (SGU_CHUNK, HEADS_PER_HALF * SGU_CHUNK), 0)
    wcol = lax.broadcasted_iota(jnp.int32, (SGU_CHUNK, HEADS_PER_HALF * SGU_CHUNK), 1) % SGU_CHUNK
    for half in range(2):
        cols = slice(half * HW, (half + 1) * HW)
        w_causal = jnp.where(wcol <= wrow, sw_ref[half], 0.0).astype(BF16)
        bias = sb_ref[:, cols]
        for c in range(T // SGU_CHUNK):
            rows = slice(c * SGU_CHUNK, (c + 1) * SGU_CHUNK)
            gvc = gv_ref[rows, cols]
            stacked = jnp.concatenate(
                [jnp.where(lane_head == j, gvc, jnp.zeros_like(gvc)) for j in range(HEADS_PER_HALF)], axis=0)
            mixed = jnp.dot(w_causal, stacked, preferred_element_type=F32) + bias
            yc_ref[0, rows, cols] = (u_ref[rows, cols] * mixed).astype(BF16)


def _odd_in(h, layer, g, w_in, sgu_norm, sgu_w, sgu_b, fox_bf):
    B, S, D = h.shape
    T, W = OD_T, MIX_W
    wf = jnp.pad(w_in[layer, :, 5 * W:], ((0, 0), (0, F_PAD - (w_in.shape[2] - 5 * W)))).astype(BF16)
    seg =_block_diag(jnp.ones((N_HEADS, HEAD_DIM, HEAD_DIM), F32)).astype(BF16)
    sw = sgu_w[layer].reshape(2, HEADS_PER_HALF, SGU_CHUNK, SGU_CHUNK).transpose(0, 2, 1, 3).reshape(
        2, SGU_CHUNK, HEADS_PER_HALF * SGU_CHUNK)
    sb = jnp.repeat(sgu_b[layer].T, HEAD_DIM, axis=1)
    bf = jnp.pad(fox_bf[layer], (0, F_PAD - N_HEADS)).reshape(1, F_PAD)
    (g_spec, g), (gn_spec, sgu_norm) = _layer_row(g, layer), _layer_row(sgu_norm, layer)
    tile = pl.BlockSpec((1, T, D), lambda b, t: (b, t, 0))
    half_tile = pl.BlockSpec((1, T, W), lambda b, t: (b, t, 0))
    head_tile = pl.BlockSpec((1, N_HEADS, T, PAIR_W), lambda b, t: (b, 0, t, 0))
    act = jax.ShapeDtypeStruct((B, S, W), BF16)
    per_head = jax.ShapeDtypeStruct((B, N_HEADS, S, PAIR_W), BF16)
    return pl.pallas_call(
        functools.partial(_odd_in_kernel, layer=layer),
        out_shape=(act, per_head, per_head, per_head, jax.ShapeDtypeStruct((B, S // T, 8, PAIR_W), F32)),
        grid=(B, S // T),
        in_specs=[tile, g_spec, pl.BlockSpec(memory_space=pl.ANY), _resident((D, F_PAD)), gn_spec,
                  _resident((W, W)), _resident((2, SGU_CHUNK, HEADS_PER_HALF * SGU_CHUNK)),
                  _resident((SGU_CHUNK, W)), _resident((1, F_PAD))],
        out_specs=(half_tile, head_tile, head_tile, head_tile,
                   pl.BlockSpec((1, 1, 8, PAIR_W), lambda b, t: (b, t, 0, 0))),
        scratch_shapes=[pltpu.VMEM((T, D), BF16),
                        pltpu.VMEM((T, W), BF16),
                        pltpu.VMEM((T, W), F32),
                        pltpu.VMEM((N_HEADS, 128), F32),
                        pltpu.VMEM((D, 5 * W), BF16),
                        pltpu.VMEM((2, D, W), F32),
                        pltpu.SemaphoreType.DMA((2,))],
        compiler_params=pltpu.CompilerParams(dimension_semantics=("arbitrary", "arbitrary"),
                                             vmem_limit_bytes=VMEM_LIMIT_BYTES),
        name="odd_in",
    )(h, g, w_in, wf, sgu_norm, seg, sw, sb, bf)


AT_T = 512
AT_K = 256
KEY_BLOCKS_PER_TILE = AT_T // AT_K
M_INIT = -1e30
QK_AHEAD = 2
SKIP_LOG2 = -160.0
NORM_SLACK = 1.05


def _first_needed_block(stats):
    KB = KEY_BLOCKS_PER_TILE
    batch, n = stats.shape[:2]
    c_first = stats[:, :, 0, :N_HEADS]
    c_end = jnp.stack([stats[:, :, 4, :N_HEADS], stats[:, :, 1, :N_HEADS]], axis=2).reshape(batch, KB * n, N_HEADS)
    q_norm = stats[:, :, 2, 0]
    k_norm = jnp.max(stats[:, :, 3, 0], axis=1, keepdims=True)
    spread = NORM_SLACK * 2.0 * q_norm * k_norm
    decay = (c_end[:, None, :, :] - c_first[:, :, None, :]) * LOG2E
    block = lax.broadcasted_iota(jnp.int32, (n, KB * n), 1)
    tile = lax.broadcasted_iota(jnp.int32, (n, KB * n), 0)
    worst = jnp.max(spread[:, :, None, None] - decay, axis=-1)
    needed = (worst >= SKIP_LOG2) | (block >= KB * tile)[None]
    return jnp.min(jnp.where(needed, block[None], KB * n), axis=-1).astype(jnp.int32)


def _attn_kernel(first_ref, q_ref, kprev_ref, kcur_ref, vprev_ref, vcur_ref, k_hbm, v_hbm, yc_ref, h_ref, wout_ref,
                 o_ref, s_ref, mx_ref, m_ref, acc_ref, yd_ref, kfar_ref, vfar_ref, sem_far):
    T, K, W, KB = AT_T, AT_K, MIX_W, KEY_BLOCKS_PER_TILE
    qi = pl.program_id(1)

    def run_blocks(blocks):
        assert all(blk[2] == 0 for blk in blocks if blk[4] in ('init', 'flip'))
        items = [(blk, hh) for blk in blocks for hh in range(N_HEADS)]

        def logits(item):
            (keys, _, r0, mask, mode), hh = item
            rows = slice(r0, T)
            s = lax.dot_general(q_ref[0, hh, rows, :], keys(hh), (((1,), (1,)), ((), ())),
                                preferred_element_type=F32)
            if mask == 'diagonal':
                visible = (lax.broadcasted_iota(jnp.int32, (T - r0, K), 1)
                           <= lax.broadcasted_iota(jnp.int32, (T - r0, K), 0))
                s = jnp.where(visible, s, -jnp.inf)
            elif mask == 'if_not_first_tile':
                s = jnp.where(qi > 0, s, -jnp.inf)
            s_ref[hh, rows, :] = s
            row_max = jnp.broadcast_to(jnp.max(s, axis=-1, keepdims=True), (T - r0, PAIR_W))
            if mode == 'init':
                mx_ref[hh, rows, :] = jnp.maximum(row_max, M_INIT)
            elif mode == 'flip':
                m_ref[hh, rows, :] = jnp.maximum(mx_ref[hh, rows, :], row_max)
            else:
                mx_ref[hh, rows, :] = jnp.maximum(m_ref[hh, rows, :], row_max)

        def update(item):
            (_, values, r0, _, mode), hh = item
            rows = slice(r0, T)
            old_ref, new_ref = (mx_ref, m_ref) if mode == 'flip' else (m_ref, mx_ref)
            m_new = new_ref[hh, rows, :]
            p = jnp.exp2(s_ref[hh, rows, :] - jnp.concatenate([m_new] * (K // PAIR_W), axis=1))
            pv = jnp.dot(p.astype(BF16), values(hh), preferred_element_type=F32)
            if mode == 'init':
                acc_ref[hh, rows, :] = pv
            else:
                acc_ref[hh, rows, :] = jnp.exp2(old_ref[hh, rows, :] - m_new) * acc_ref[hh, rows, :] + pv
            if mode == 'keep':
                m_ref[hh, rows, :] = m_new

        for item in items[:QK_AHEAD]:
            logits(item)
        for i, item in enumerate(items):
            if i + QK_AHEAD < len(items):
                logits(items[i + QK_AHEAD])
            update(item)

    def tile_half(k_ref, v_ref, half):
        rows = slice(half * K, (half + 1) * K)
        return (lambda hh: k_ref[0, hh, rows, :]), (lambda hh: v_ref[0, hh, rows, :])

    def far_block(kb, carry):
        bb = pl.program_id(0)
        k0 = pl.multiple_of(kb * K, K)
        copies = [pltpu.make_async_copy(k_hbm.at[bb, :, pl.ds(k0, K), :], kfar_ref, sem_far.at[0]),
                  pltpu.make_async_copy(v_hbm.at[bb, :, pl.ds(k0, K), :], vfar_ref, sem_far.at[1])]
        for c in copies:
            c.start()
        for c in copies:
            c.wait()
        run_blocks([((lambda hh: kfar_ref[hh]), (lambda hh: vfar_ref[hh]), 0, None, 'keep')])
        return carry

    run_blocks([tile_half(kprev_ref, vprev_ref, KB - 1) + (0, 'if_not_first_tile', 'init')]
               + [tile_half(kcur_ref, vcur_ref, d) + (d * K, 'diagonal', 'flip' if d == 0 else 'keep')
                  for d in range(KB)])
    lax.fori_loop(first_ref[pl.program_id(0), qi], KB * qi - 1, far_block, 0)

    low = lax.broadcasted_iota(jnp.int32, (T, PAIR_W), 1) < HEAD_DIM
    for pair in range(N_PAIRS):
        acc_a = acc_ref[2 * pair]
        acc_b = acc_ref[2 * pair + 1]
        y = jnp.where(low, acc_a / acc_a[:, HEAD_DIM:HEAD_DIM + 1], acc_b / acc_b[:, 0:1])
        yd_ref[:, pair * PAIR_W:(pair + 1) * PAIR_W] = y.astype(BF16)
    o_ref[0] = (h_ref[0]
                + jnp.dot(yc_ref[0], wout_ref[0:W, :], preferred_element_type=F32)
                + jnp.dot(yd_ref[...], wout_ref[W:2 * W, :], preferred_element_type=F32))


def _attention(q, k, v, stats, yc, h, layer, w_out):
    B, S, D = h.shape
    T, K, W = AT_T, AT_K, MIX_W
    assert OD_T == AT_T and KEY_BLOCKS_PER_TILE == 2, "tile summaries hold c at the middle and end of an odd_in tile"
    q_tile = pl.BlockSpec((1, N_HEADS, T, PAIR_W), lambda b, t: (b, 0, t, 0))
    prev_tile = pl.BlockSpec((1, N_HEADS, T, PAIR_W), lambda b, t: (b, 0, jnp.maximum(t - 1, 0), 0))
    hbm = pl.BlockSpec(memory_space=pl.ANY)
    tile = pl.BlockSpec((1, T, D), lambda b, t: (b, t, 0))
    return pl.pallas_call(
        _attn_kernel,
        out_shape=jax.ShapeDtypeStruct((B, S, D), F32),
        grid=(B, S // T),
        in_specs=[pl.BlockSpec(memory_space=pltpu.SMEM), q_tile, prev_tile, q_tile, prev_tile, q_tile, hbm, hbm,
                  pl.BlockSpec((1, T, W), lambda b, t: (b, t, 0)), tile, _layer_block((2 * W, D), layer)],
        out_specs=tile,
        scratch_shapes=[pltpu.VMEM((N_HEADS, T, K), F32),
                        pltpu.VMEM((N_HEADS, T, PAIR_W), F32),
                        pltpu.VMEM((N_HEADS, T, PAIR_W), F32),
                        pltpu.VMEM((N_HEADS, T, PAIR_W), F32),
                        pltpu.VMEM((T, W), BF16),
                        pltpu.VMEM((N_HEADS, K, PAIR_W), BF16),
                        pltpu.VMEM((N_HEADS, K, PAIR_W), BF16),
                        pltpu.SemaphoreType.DMA((2,))],
        compiler_params=pltpu.CompilerParams(dimension_semantics=("arbitrary", "arbitrary"),
                                             vmem_limit_bytes=VMEM_LIMIT_BYTES),
        name="fox_attention",
    )(_first_needed_block(stats), q, k, k, v, v, k, v, yc, h, w_out)


def kernel(x, mix0_norm, mix0_w_in, lru_conv_w, lru_conv_b, lru_wa, lru_ba, lru_wx, lru_bx, lru_lambda,
           sconv_w, sconv_b, mix0_w_out, mix1_norm, mix1_w_in, sgu_norm, sgu_w, sgu_b, fox_bf, mix1_w_out,
           ffn_norm, ffn_up, ffn_conv_w, ffn_conv_b, ffn_down, final_norm):
    h = _even_mixer(x, 0, mix0_norm, mix0_w_in, lru_conv_w, lru_conv_b, lru_wa, lru_ba, lru_wx, lru_bx,
                    lru_lambda, sconv_w, sconv_b, mix0_w_out)
    h = _ffn(h, 0, ffn_norm, ffn_up, ffn_conv_w, ffn_conv_b, ffn_down, final_norm, final_norm=False)
    yc, q, k, v, stats = _odd_in(h, 0, mix1_norm, mix1_w_in, sgu_norm, sgu_w, sgu_b, fox_bf)
    h = _attention(q, k, v, stats, yc, h, 0, mix1_w_out.astype(BF16))
    return _ffn(h, 1, ffn_norm, ffn_up, ffn_conv_w, ffn_conv_b, ffn_down, final_norm, final_norm=True)
```
